```python
import math
import jax, jax.numpy as jnp
from jax import lax
import numpy as np

D_MODEL = 1024
BATCH = 4
SEQ = 4096
DEPTH = 4
DEC_BATCH = 128
DEC_SEQ = 8
PAST_LEN = 2048
PAGE_SIZE = 128

POOL_WIDTH = D_MODEL // 4
POOL_WINDOWS = (2, 4, 8, 16)
N_POOL_GROUPS = len(POOL_WINDOWS)
POOL_GROUP = POOL_WIDTH // N_POOL_GROUPS
POOL_STATE = max(POOL_WINDOWS) - 1
HEAD_DIM = 64
ATTN_WIDTH = D_MODEL - POOL_WIDTH
N_HEADS = ATTN_WIDTH // HEAD_DIM
DILATED_BRANCHES = ((128, 1), (512, 4), (2048, 16))
WIN_MAX = max(w for w, _ in DILATED_BRANCHES)
BLK = 128
D_FF = 2816
RMS_EPS = 1e-6
NEG = -1e30

kernel_name = "hymba_pool_dilated_macaron_step"


def _alibi_slopes(n):
    def pow2(m):
        start = 2.0 ** (-8.0 / m)
        return [start ** (i + 1) for i in range(m)]
    if math.log2(n).is_integer():
        s = pow2(n)
    else:
        c = 2 ** int(math.floor(math.log2(n)))
        s = pow2(c) + pow2(2 * c)[0::2][: n - c]
    return jnp.asarray(np.array(s, dtype=np.float32))


def rmsnorm(x, g):
    xf = x.astype(jnp.float32)
    y = xf * lax.rsqrt(jnp.mean(xf * xf, axis=-1, keepdims=True) + RMS_EPS)
    return (y * g.astype(jnp.float32)).astype(x.dtype)


def swiglu(x, wg, wu, wd):
    return (jax.nn.silu(x @ wg) * (x @ wu)) @ wd


def pool_mix(u, u_prev, pos0, w_pool, scale):
    B, T, C = u.shape
    f32 = jnp.float32
    full = jnp.concatenate([u_prev.astype(f32), u.astype(f32)], axis=1)
    cs = jnp.concatenate([jnp.zeros((B, 1, C), f32), jnp.cumsum(full, axis=1)], axis=1)
    pos = pos0 + jnp.arange(T)
    end = POOL_STATE + 1 + jnp.arange(T)
    means = []
    for gi, w in enumerate(POOL_WINDOWS):
        csg = cs[..., gi * POOL_GROUP:(gi + 1) * POOL_GROUP]
        cnt = jnp.minimum(w, pos + 1).astype(f32)[None, :, None]
        means.append((csg[:, end] - csg[:, end - w]) / cnt)
    diff = (jnp.concatenate(means, axis=-1) - u.astype(f32)).astype(u.dtype)
    diff = diff.reshape(B, T, N_POOL_GROUPS, POOL_GROUP)
    y = jnp.einsum('btgc,gcd->btgd', diff, w_pool).reshape(B, T, C)
    return y * scale


def dilated_branch_prompt(q, k, v, window, dil, slopes):
    B, T, H, Dh = q.shape
    f32 = jnp.float32
    L = window // dil
    n = T // dil
    nb = -(-n // BLK)
    npad = nb * BLK

    def to_sub(a):
        a = a.reshape(B, n, dil, H, Dh).transpose(0, 2, 1, 3, 4).reshape(B * dil, n, H, Dh)
        a = jnp.pad(a, ((0, 0), (0, npad - n), (0, 0), (0, 0)))
        return a.reshape(B * dil, nb, BLK, H, Dh)

    def with_prev(a):
        prev = jnp.pad(a[:, :-1], ((0, 0), (1, 0), (0, 0), (0, 0), (0, 0)))
        return jnp.concatenate([prev, a], axis=2)

    def from_sub(a):
        rest = a.shape[3:]
        a = a.reshape((B, dil, npad) + rest)[:, :, :n]
        return jnp.moveaxis(a, 1, 2).reshape((B, T) + rest)

    qs = to_sub(q)
    kk = with_prev(to_sub(k))
    vv = with_prev(to_sub(v))
    s = jnp.einsum('bnqhd,bnkhd->bnhqk', qs, kk).astype(f32) * (Dh ** -0.5)
    qi = jnp.arange(BLK)[:, None]
    ki = jnp.arange(2 * BLK)[None, :]
    delta = BLK + qi - ki
    jkey = (jnp.arange(nb)[:, None, None] - 1) * BLK + ki[None]
    valid = (delta >= 0) & (delta <= L) & (jkey >= 0)
    bias = -slopes[:, None, None] * (delta * dil).astype(f32)[None]
    s = jnp.where(valid[None, :, None], s + bias[None, None], NEG)
    m = jnp.max(s, axis=-1, keepdims=True)
    p = jnp.exp(s - m)
    l = jnp.sum(p, axis=-1, keepdims=True)
    o = jnp.einsum('bnhqk,bnkhd->bnqhd', p, vv.astype(f32)) / jnp.transpose(l, (0, 1, 3, 2, 4))
    lse = jnp.transpose((m + jnp.log(l))[..., 0], (0, 1, 3, 2))
    return from_sub(lse), from_sub(o)


def dilated_branch_sample(q, k_all, v_all, R, window, dil, slopes):
    B, S, H, Dh = q.shape
    f32 = jnp.float32
    L = window // dil
    dist = jnp.arange(L + 1) * dil
    rows = R + jnp.arange(S)[:, None] - dist[None, :]
    valid = rows >= 0
    idx = jnp.maximum(rows, 0)
    kg = k_all[:, idx]
    vg = v_all[:, idx]
    s = jnp.einsum('bqhd,bqkhd->bhqk', q, kg).astype(f32) * (Dh ** -0.5)
    bias = -slopes[:, None, None] * dist.astype(f32)[None, None, :]
    s = jnp.where(valid[None, None], s + bias[None], NEG)
    m = jnp.max(s, axis=-1, keepdims=True)
    p = jnp.exp(s - m)
    l = jnp.sum(p, axis=-1, keepdims=True)
    o = jnp.einsum('bhqk,bqkhd->bqhd', p, vg.astype(f32)) / jnp.transpose(l, (0, 2, 1, 3))
    lse = jnp.transpose((m + jnp.log(l))[..., 0], (0, 2, 1))
    return lse, o


def merge_by_denominators(results):
    lse = jnp.stack([r[0] for r in results], axis=0)
    o = jnp.stack([r[1] for r in results], axis=0)
    w = jax.nn.softmax(lse, axis=0)
    return jnp.sum(w[..., None] * o, axis=0)


def split_proj(z):
    B, T, _ = z.shape
    u = z[..., :POOL_WIDTH]
    q, k, v = jnp.split(z[..., POOL_WIDTH:], 3, axis=-1)
    shp = (B, T, N_HEADS, HEAD_DIM)
    return u, q.reshape(shp), k.reshape(shp), v.reshape(shp)


def layer(x, i, slopes, pool_prev, pos0, kv_prev,
          g_ffn1, w_gate1, w_up1, w_down1, g_mix, w_in, w_pool, pool_scale, w_out,
          g_ffn2, w_gate2, w_up2, w_down2):
    B, T, _ = x.shape
    x = x + 0.5 * swiglu(rmsnorm(x, g_ffn1[i]), w_gate1[i], w_up1[i], w_down1[i])
    h = rmsnorm(x, g_mix[i])
    u, q, k, v = split_proj(h @ w_in[i])
    if pool_prev is None:
        pool_prev = jnp.zeros((B, POOL_STATE, POOL_WIDTH), u.dtype)
    y_pool = pool_mix(u, pool_prev, pos0, w_pool[i], pool_scale[i])
    if kv_prev is None:
        res = [dilated_branch_prompt(q, k, v, w, d, slopes) for (w, d) in DILATED_BRANCHES]
    else:
        ck, cv = kv_prev
        R = ck.shape[1]
        k_all = jnp.concatenate([ck.astype(k.dtype), k], axis=1)
        v_all = jnp.concatenate([cv.astype(v.dtype), v], axis=1)
        res = [dilated_branch_sample(q, k_all, v_all, R, w, d, slopes) for (w, d) in DILATED_BRANCHES]
    y_att = merge_by_denominators(res).astype(x.dtype).reshape(B, T, ATTN_WIDTH)
    x = x + jnp.concatenate([y_pool, y_att], axis=-1) @ w_out[i]
    x = x + 0.5 * swiglu(rmsnorm(x, g_ffn2[i]), w_gate2[i], w_up2[i], w_down2[i])
    new_pool = jnp.concatenate([pool_prev.astype(u.dtype), u], axis=1)[:, -POOL_STATE:]
    return x, k, v, new_pool


def setup_inputs(seed: int = 0) -> dict:
    key = jax.random.key(seed)
    ks = jax.random.split(key, 20)
    f = jnp.float32
    R = min(WIN_MAX, PAST_LEN)
    D, F, ZW = D_MODEL, D_FF, POOL_WIDTH + 3 * ATTN_WIDTH

    def nrm(k, shape, scale):
        return jax.random.normal(k, shape, f) * scale

    def gain(k, shape):
        return 1.0 + 0.02 * jax.random.normal(k, shape, f)

    return {
        "x_prompt": nrm(ks[0], (BATCH, SEQ, D), 1.0),
        "x_sample": nrm(ks[1], (DEC_BATCH, DEC_SEQ, D), 1.0),
        "cache_k": nrm(ks[2], (DEPTH, DEC_BATCH, R, N_HEADS, HEAD_DIM), 1.0),
        "cache_v": nrm(ks[3], (DEPTH, DEC_BATCH, R, N_HEADS, HEAD_DIM), 1.0),
        "state_pool": nrm(ks[4], (DEPTH, DEC_BATCH, POOL_STATE, POOL_WIDTH), 1.0),
        "g_ffn1": gain(ks[5], (DEPTH, D)),
        "w_gate1": nrm(ks[6], (DEPTH, D, F), D ** -0.5),
        "w_up1": nrm(ks[7], (DEPTH, D, F), D ** -0.5),
        "w_down1": nrm(ks[8], (DEPTH, F, D), F ** -0.5),
        "g_mix": gain(ks[9], (DEPTH, D)),
        "w_in": nrm(ks[10], (DEPTH, D, ZW), D ** -0.5),
        "w_pool": nrm(ks[11], (DEPTH, N_POOL_GROUPS, POOL_GROUP, POOL_GROUP), POOL_GROUP ** -0.5),
        "pool_scale": 1.0 + 0.1 * jax.random.normal(ks[12], (DEPTH, POOL_WIDTH), f),
        "w_out": nrm(ks[13], (DEPTH, D, D), D ** -0.5),
        "g_ffn2": gain(ks[14], (DEPTH, D)),
        "w_gate2": nrm(ks[15], (DEPTH, D, F), D ** -0.5),
        "w_up2": nrm(ks[16], (DEPTH, D, F), D ** -0.5),
        "w_down2": nrm(ks[17], (DEPTH, F, D), F ** -0.5),
        "g_final": gain(ks[18], (D,)),
    }


def reference(x_prompt, x_sample, cache_k, cache_v, state_pool,
              g_ffn1, w_gate1, w_up1, w_down1, g_mix, w_in, w_pool, pool_scale, w_out,
              g_ffn2, w_gate2, w_up2, w_down2, g_final):
    slopes = _alibi_slopes(N_HEADS)
    weights = (g_ffn1, w_gate1, w_up1, w_down1, g_mix, w_in, w_pool, pool_scale, w_out,
               g_ffn2, w_gate2, w_up2, w_down2)
    keep_p = min(WIN_MAX, x_prompt.shape[1])
    xp, xs = x_prompt, x_sample
    kp, vp, pp, ksl, vsl, psl = [], [], [], [], [], []
    for i in range(DEPTH):
        xp, k_new, v_new, pool_new = layer(xp, i, slopes, None, 0, None, *weights)
        kp.append(k_new[:, -keep_p:])
        vp.append(v_new[:, -keep_p:])
        pp.append(pool_new)
        xs, k_new, v_new, pool_new = layer(xs, i, slopes, state_pool[i], PAST_LEN,
                                           (cache_k[i], cache_v[i]), *weights)
        ksl.append(k_new)
        vsl.append(v_new)
        psl.append(pool_new)
    y_prompt = rmsnorm(xp, g_final)
    y_sample = rmsnorm(xs, g_final)
    return (y_prompt, y_sample, jnp.stack(kp), jnp.stack(vp), jnp.stack(pp),
            jnp.stack(ksl), jnp.stack(vsl), jnp.stack(psl))
```

```python
import functools
import math

import jax
import jax.numpy as jnp
import numpy as np
from jax import lax
from jax.experimental import pallas as pl
from jax.experimental.pallas import tpu as pltpu

F32 = jnp.float32
BF16 = jnp.bfloat16

D_MODEL = 1024
D_FF = 2816
POOL_WIDTH = 256
POOL_WINDOWS = (2, 4, 8, 16)
POOL_GROUP = POOL_WIDTH // len(POOL_WINDOWS)
POOL_STATE = max(POOL_WINDOWS) - 1
POOL_HALO = 16
HEAD_DIM = 64
ATTN_WIDTH = D_MODEL - POOL_WIDTH
N_HEADS = ATTN_WIDTH // HEAD_DIM
BRANCHES = ((128, 1), (512, 4), (2048, 16))
WIN_MAX = max(w for w, _ in BRANCHES)
PAST_LEN = 2048
BLK = 128
RMS_EPS = 1e-6
NEG = -1e30
LANES = 128
FF_CHUNK = 256
VMEM_LIMIT = 56 * 1024 * 1024


def _alibi_slopes(n):
    def pow2(m):
        start = 2.0 ** (-8.0 / m)
        return [start ** (i + 1) for i in range(m)]
    if math.log2(n).is_integer():
        s = pow2(n)
    else:
        c = 2 ** int(math.floor(math.log2(n)))
        s = pow2(c) + pow2(2 * c)[0::2][: n - c]
    return np.array(s, dtype=np.float32)


def _params(n_axes):
    return pltpu.CompilerParams(
        dimension_semantics=("arbitrary",) * n_axes,
        vmem_limit_bytes=VMEM_LIMIT)


def _resident(shape):
    nd = len(shape)
    return pl.BlockSpec(shape, lambda *_: (0,) * nd,
                        pipeline_mode=pl.Buffered(1))


def _rms(x, g):
    var = jnp.mean(x * x, axis=-1, keepdims=True)
    return x * lax.rsqrt(var + RMS_EPS) * g


def _ffn_kernel(x_ref, g_ref, wg_ref, wu_ref, wd_ref, *rest, final):
    if final:
        gf_ref, o_ref, a_ref = rest
    else:
        o_ref, a_ref = rest
    x = x_ref[...]
    h = _rms(x, g_ref[...]).astype(BF16)
    for c in range(D_FF // FF_CHUNK):
        sl = slice(c * FF_CHUNK, (c + 1) * FF_CHUNK)
        gate = jnp.dot(h, wg_ref[:, sl], preferred_element_type=F32)
        up = jnp.dot(h, wu_ref[:, sl], preferred_element_type=F32)
        a_ref[:, sl] = (gate * jax.nn.sigmoid(gate) * up).astype(BF16)
    y = jnp.dot(a_ref[...], wd_ref[...], preferred_element_type=F32)
    xn = x + 0.5 * y
    if final:
        xn = _rms(xn, gf_ref[...])
    o_ref[...] = xn


def _ffn(x, g, wg, wu, wd, g_final=None, *, tm):
    m = x.shape[0]
    final = g_final is not None
    row = pl.BlockSpec((tm, D_MODEL), lambda i: (i, 0))
    in_specs = [row, _resident((1, D_MODEL)), _resident((D_MODEL, D_FF)),
                _resident((D_MODEL, D_FF)), _resident((D_FF, D_MODEL))]
    args = [x, g, wg, wu, wd]
    if final:
        in_specs.append(_resident((1, D_MODEL)))
        args.append(g_final)
    return pl.pallas_call(
        functools.partial(_ffn_kernel, final=final),
        grid=(m // tm,),
        in_specs=in_specs,
        out_specs=row,
        out_shape=jax.ShapeDtypeStruct((m, D_MODEL), F32),
        scratch_shapes=[pltpu.VMEM((tm, D_FF), BF16)],
        compiler_params=_params(1),
        name="ffn_final" if final else "ffn",
    )(*args)


def _inproj_kernel(x_ref, g_ref, w_ref, u_ref, q_ref, k_ref, v_ref, kf_ref, vf_ref):
    h = _rms(x_ref[...], g_ref[...]).astype(BF16)
    a = POOL_WIDTH
    u_ref[...] = jnp.dot(h, w_ref[:, :a], preferred_element_type=F32)
    q = jnp.dot(h, w_ref[:, a:a + ATTN_WIDTH], preferred_element_type=F32)
    q_ref[...] = (q * HEAD_DIM ** -0.5).astype(BF16)
    k = jnp.dot(h, w_ref[:, a + ATTN_WIDTH:a + 2 * ATTN_WIDTH], preferred_element_type=F32)
    kf_ref[...] = k
    k_ref[...] = k.astype(BF16)
    v = jnp.dot(h, w_ref[:, a + 2 * ATTN_WIDTH:], preferred_element_type=F32)
    vf_ref[...] = v
    v_ref[...] = v.astype(BF16)


def _inproj(x, g, w, *, tm):
    m = x.shape[0]
    zw = POOL_WIDTH + 3 * ATTN_WIDTH
    def row(n):
        return pl.BlockSpec((tm, n), lambda i: (i, 0))
    return pl.pallas_call(
        _inproj_kernel,
        grid=(m // tm,),
        in_specs=[row(D_MODEL), _resident((1, D_MODEL)), _resident((D_MODEL, zw))],
        out_specs=[row(POOL_WIDTH), row(ATTN_WIDTH), row(ATTN_WIDTH), row(ATTN_WIDTH),
                   row(ATTN_WIDTH), row(ATTN_WIDTH)],
        out_shape=[jax.ShapeDtypeStruct((m, POOL_WIDTH), F32),
                   jax.ShapeDtypeStruct((m, ATTN_WIDTH), BF16),
                   jax.ShapeDtypeStruct((m, ATTN_WIDTH), BF16),
                   jax.ShapeDtypeStruct((m, ATTN_WIDTH), BF16),
                   jax.ShapeDtypeStruct((m, ATTN_WIDTH), F32),
                   jax.ShapeDtypeStruct((m, ATTN_WIDTH), F32)],
        compiler_params=_params(1),
        name="inproj",
    )(x, g, w)


def _pool_mix(ext_ref, rows, pos, w_ref, scale_ref):
    u = ext_ref[pl.ds(POOL_HALO, rows), :]
    lane = lax.broadcasted_iota(jnp.int32, (1, POOL_WIDTH), 1)
    acc = u
    mean = jnp.zeros_like(u)
    for j in range(1, max(POOL_WINDOWS)):
        acc = acc + ext_ref[pl.ds(POOL_HALO - j, rows), :]
        w = j + 1
        if w in POOL_WINDOWS:
            gi = POOL_WINDOWS.index(w)
            cnt = jnp.minimum(w, pos + 1).astype(F32)
            in_group = (lane >= gi * POOL_GROUP) & (lane < (gi + 1) * POOL_GROUP)
            mean = jnp.where(in_group, acc / cnt, mean)
    diff = (mean - u).astype(BF16)
    y = jnp.dot(diff, w_ref[...], preferred_element_type=F32)
    return y * scale_ref[...]


def _pool_prompt_kernel(u_ref, halo_ref, w_ref, scale_ref, y_ref, ext_ref, *, tp):
    t = pl.program_id(1)
    ext_ref[pl.ds(0, POOL_HALO), :] = jnp.where(t == 0, 0.0, halo_ref[0])
    ext_ref[pl.ds(POOL_HALO, tp), :] = u_ref[0]
    pos = t * tp + lax.broadcasted_iota(jnp.int32, (tp, 1), 0)
    y_ref[0] = _pool_mix(ext_ref, tp, pos, w_ref, scale_ref).astype(BF16)


def _pool_prompt(u, w_bd, scale, *, tp):
    b, t, _ = u.shape
    hb = tp // POOL_HALO
    return pl.pallas_call(
        functools.partial(_pool_prompt_kernel, tp=tp),
        grid=(b, t // tp),
        in_specs=[pl.BlockSpec((1, tp, POOL_WIDTH), lambda bi, ti: (bi, ti, 0)),
                  pl.BlockSpec((1, POOL_HALO, POOL_WIDTH),
                               lambda bi, ti: (bi, jnp.maximum(ti * hb - 1, 0), 0)),
                  _resident((POOL_WIDTH, POOL_WIDTH)), _resident((1, POOL_WIDTH))],
        out_specs=pl.BlockSpec((1, tp, POOL_WIDTH), lambda bi, ti: (bi, ti, 0)),
        out_shape=jax.ShapeDtypeStruct((b, t, POOL_WIDTH), BF16),
        scratch_shapes=[pltpu.VMEM((tp + POOL_HALO, POOL_WIDTH), F32)],
        compiler_params=_params(2),
        name="pool_prompt",
    )(u, u, w_bd, scale)


def _pool_sample_kernel(ext_ref, w_ref, scale_ref, y_ref, *, rows, pos0):
    pos = jnp.full((rows, 1), pos0, jnp.int32)
    y_ref[...] = _pool_mix(ext_ref, rows, pos, w_ref, scale_ref).astype(BF16)


def _pool_sample(ext, w_bd, scale, *, pos0):
    assert pos0 >= POOL_STATE
    rows = ext.shape[0] - POOL_HALO
    return pl.pallas_call(
        functools.partial(_pool_sample_kernel, rows=rows, pos0=pos0),
        grid=(1,),
        in_specs=[_resident(ext.shape), _resident((POOL_WIDTH, POOL_WIDTH)),
                  _resident((1, POOL_WIDTH))],
        out_specs=pl.BlockSpec((rows, POOL_WIDTH), lambda i: (0, 0)),
        out_shape=jax.ShapeDtypeStruct((rows, POOL_WIDTH), BF16),
        compiler_params=_params(1),
        name="pool_sample",
    )(ext, w_bd, scale)


def _attn_prompt_kernel(*refs, first, last):
    q_ref, kp_ref, kc_ref, vp_ref, vc_ref, tab_ref = refs[:6]
    refs = refs[6:]
    if not first:
        acc_in, m_in, l_in = refs[:3]
        refs = refs[3:]
    if last:
        (o_ref,) = refs
    else:
        acc_out, m_out, l_out = refs
    j = pl.program_id(2)
    col = lax.broadcasted_iota(jnp.int32, (1, 2 * BLK), 1)
    pen = jnp.where((col < BLK) & (j == 0), NEG, 0.0)
    lane = lax.broadcasted_iota(jnp.int32, (1, LANES), 1)
    low = lane < HEAD_DIM
    m_tile = jnp.zeros((BLK, LANES), F32)
    l_tile = jnp.zeros((BLK, LANES), F32)
    for hp in range(N_HEADS // 2):
        sl = slice(hp * LANES, (hp + 1) * LANES)
        q2 = q_ref[0, :, sl]
        k2 = jnp.concatenate([kp_ref[0, :, sl], kc_ref[0, :, sl]], axis=0)
        v2 = jnp.concatenate([vp_ref[0, :, sl], vc_ref[0, :, sl]], axis=0)
        outs, alphas, ls = [], [], []
        for sub in range(2):
            h = 2 * hp + sub
            qh = jnp.where(low if sub == 0 else ~low, q2, jnp.zeros_like(q2))
            s = lax.dot_general(qh, k2, (((1,), (1,)), ((), ())),
                                preferred_element_type=F32)
            s = s + tab_ref[h] + pen
            m_new = jnp.max(s, axis=-1, keepdims=True)
            if not first:
                m_prev = m_in[0, :, h:h + 1]
                m_new = jnp.maximum(m_new, m_prev)
            p = jnp.exp(s - m_new)
            l_new = jnp.sum(p, axis=-1, keepdims=True)
            o = jnp.dot(p.astype(BF16), v2, preferred_element_type=F32)
            if not first:
                alpha = jnp.exp(m_prev - m_new)
                l_new = l_new + alpha * l_in[0, :, h:h + 1]
                alphas.append(alpha)
            outs.append(o)
            ls.append(l_new)
            m_tile = jnp.where(lane == h, m_new, m_tile)
            l_tile = jnp.where(lane == h, l_new, l_tile)
        o2 = jnp.where(low, outs[0], outs[1])
        if not first:
            o2 = o2 + jnp.where(low, alphas[0], alphas[1]) * acc_in[0, :, sl]
        if last:
            o_ref[0, :, sl] = (o2 / jnp.where(low, ls[0], ls[1])).astype(BF16)
        else:
            acc_out[0, :, sl] = o2
    if not last:
        m_out[0] = m_tile
        l_out[0] = l_tile


def _branch_table(dil, slopes):
    qi = np.arange(BLK)[:, None]
    ki = np.arange(2 * BLK)[None, :]
    delta = BLK + qi - ki
    valid = (delta >= 0) & (delta <= BLK)
    bias = -slopes[:, None, None] * (delta * dil).astype(np.float32)[None]
    return np.where(valid[None], bias, np.float32(NEG)).astype(np.float32)


def _attn_prompt(q, k, v, state, *, batch, dil, slopes, last):
    n_rows = q.shape[0]
    t = n_rows // batch
    nb = t // dil // BLK
    first = state is None
    view = lambda a, w: a.reshape(n_rows // dil, dil * w)
    cur = lambda w: pl.BlockSpec((1, BLK, w), lambda b, r, j: (b, j, r))
    prev = lambda w: pl.BlockSpec((1, BLK, w), lambda b, r, j: (b, jnp.maximum(j - 1, 0), r))
    v3 = lambda a, w: a.reshape(batch, t // dil, dil * w)
    table = jnp.asarray(_branch_table(dil, slopes))
    in_specs = [cur(ATTN_WIDTH), prev(ATTN_WIDTH), cur(ATTN_WIDTH), prev(ATTN_WIDTH),
                cur(ATTN_WIDTH), _resident(table.shape)]
    args = [v3(q, ATTN_WIDTH), v3(k, ATTN_WIDTH), v3(k, ATTN_WIDTH),
            v3(v, ATTN_WIDTH), v3(v, ATTN_WIDTH), table]
    if not first:
        in_specs += [cur(ATTN_WIDTH), cur(LANES), cur(LANES)]
        args += [v3(state[0], ATTN_WIDTH), v3(state[1], LANES), v3(state[2], LANES)]
    if last:
        out_specs = cur(ATTN_WIDTH)
        out_shape = jax.ShapeDtypeStruct((batch, t // dil, dil * ATTN_WIDTH), BF16)
    else:
        out_specs = [cur(ATTN_WIDTH), cur(LANES), cur(LANES)]
        out_shape = [jax.ShapeDtypeStruct((batch, t // dil, dil * ATTN_WIDTH), F32),
                     jax.ShapeDtypeStruct((batch, t // dil, dil * LANES), F32),
                     jax.ShapeDtypeStruct((batch, t // dil, dil * LANES), F32)]
    out = pl.pallas_call(
        functools.partial(_attn_prompt_kernel, first=first, last=last),
        grid=(batch, dil, nb),
        in_specs=in_specs,
        out_specs=out_specs,
        out_shape=out_shape,
        compiler_params=_params(3),
        name=f"attn_prompt_d{dil}",
    )(*args)
    if last:
        return out.reshape(n_rows, ATTN_WIDTH)
    return (out[0].reshape(n_rows, ATTN_WIDTH), out[1].reshape(n_rows, LANES),
            out[2].reshape(n_rows, LANES))


def _attn_sample_kernel(q_ref, kn_ref, vn_ref, ck_ref, cv_ref, tabc_ref, tabn_ref,
                        o_ref, *, s_new):
    rows = N_HEADS * s_new
    q = q_ref[0].astype(F32)
    qt = jnp.concatenate([q] * N_HEADS, axis=0)
    r_head = lax.broadcasted_iota(jnp.int32, (rows, ATTN_WIDTH), 0) // s_new
    c_head = lax.broadcasted_iota(jnp.int32, (rows, ATTN_WIDTH), 1) // HEAD_DIM
    own = r_head == c_head
    qm32 = jnp.where(own, qt, 0.0)
    qm = qm32.astype(BF16)
    nt = (((1,), (1,)), ((), ()))
    kc = ck_ref[0, 0].astype(BF16)
    s_c = jnp.dot(qm, kc, preferred_element_type=F32) + tabc_ref[...]
    s_n = lax.dot_general(qm32, kn_ref[0], nt, preferred_element_type=F32) + tabn_ref[...]
    m = jnp.maximum(jnp.max(s_c, axis=-1, keepdims=True),
                    jnp.max(s_n, axis=-1, keepdims=True))
    p_c = jnp.exp(s_c - m)
    p_n = jnp.exp(s_n - m)
    l = jnp.sum(p_c, axis=-1, keepdims=True) + jnp.sum(p_n, axis=-1, keepdims=True)
    o = lax.dot_general(p_c.astype(BF16), cv_ref[0, 0].astype(BF16), nt,
                        preferred_element_type=F32)
    o = o + jnp.dot(p_n, vn_ref[0], preferred_element_type=F32)
    o = jnp.where(own, o / l, 0.0)
    o_ref[0] = jnp.sum(o.reshape(N_HEADS, s_new, ATTN_WIDTH), axis=0).astype(BF16)


def _sample_tables(r_len, s_new, slopes):
    def table(dist, ok):
        cnt = np.zeros(dist.shape, np.int32)
        for w, d in BRANCHES:
            cnt += (ok & (dist % d == 0) & (dist <= w)).astype(np.int32)
        bias = (-slopes[:, None, None].astype(np.float64) * dist[None]
                + np.log(np.maximum(cnt, 1))[None])
        tab = np.where((cnt > 0)[None], bias, NEG).astype(np.float32)
        return tab.reshape(N_HEADS * s_new, dist.shape[1])
    i = np.arange(s_new)[:, None]
    dist_c = r_len + i - np.arange(r_len)[None, :]
    dist_n = i - np.arange(s_new)[None, :]
    return table(dist_c, dist_c >= 0), table(dist_n, dist_n >= 0)


def _attn_sample(q, kf, vf, cache_k, cache_v, layer, *, slopes):
    depth, b, r_len = cache_k.shape[:3]
    s_new = q.shape[0] // b
    tab_c, tab_n = _sample_tables(r_len, s_new, slopes)
    ck = jnp.transpose(cache_k, (0, 1, 3, 4, 2)).reshape(depth, b, ATTN_WIDTH, r_len)
    cv = jnp.transpose(cache_v, (0, 1, 3, 4, 2)).reshape(depth, b, ATTN_WIDTH, r_len)
    new = pl.BlockSpec((1, s_new, ATTN_WIDTH), lambda i: (i, 0, 0))
    cache = pl.BlockSpec((1, 1, ATTN_WIDTH, r_len), lambda i: (layer, i, 0, 0))
    out = pl.pallas_call(
        functools.partial(_attn_sample_kernel, s_new=s_new),
        grid=(b,),
        in_specs=[new, new, new, cache, cache,
                  _resident(tab_c.shape), _resident(tab_n.shape)],
        out_specs=new,
        out_shape=jax.ShapeDtypeStruct((b, s_new, ATTN_WIDTH), BF16),
        compiler_params=_params(1),
        name="attn_sample",
    )(q.reshape(b, s_new, ATTN_WIDTH), kf.reshape(b, s_new, ATTN_WIDTH),
      vf.reshape(b, s_new, ATTN_WIDTH), ck, cv, jnp.asarray(tab_c), jnp.asarray(tab_n))
    return out.reshape(b * s_new, ATTN_WIDTH)


def _outproj_kernel(x_ref, yp_ref, ya_ref, w_ref, o_ref):
    y = jnp.dot(yp_ref[...], w_ref[:POOL_WIDTH, :], preferred_element_type=F32)
    y = y + jnp.dot(ya_ref[...], w_ref[POOL_WIDTH:, :], preferred_element_type=F32)
    o_ref[...] = x_ref[...] + y


def _outproj(x, yp, ya, w, *, tm):
    m = x.shape[0]
    def row(n):
        return pl.BlockSpec((tm, n), lambda i: (i, 0))
    return pl.pallas_call(
        _outproj_kernel,
        grid=(m // tm,),
        in_specs=[row(D_MODEL), row(POOL_WIDTH), row(ATTN_WIDTH),
                  _resident((D_MODEL, D_MODEL))],
        out_specs=row(D_MODEL),
        out_shape=jax.ShapeDtypeStruct((m, D_MODEL), F32),
        compiler_params=_params(1),
        name="outproj",
    )(x, yp, ya, w)


def _block_diag(w_pool):
    g, c, _ = w_pool.shape
    out = jnp.zeros((g * c, g * c), w_pool.dtype)
    for gi in range(g):
        out = lax.dynamic_update_slice(out, w_pool[gi], (gi * c, gi * c))
    return out


def kernel(x_prompt, x_sample, cache_k, cache_v, state_pool, g_ffn1, w_gate1, w_up1, w_down1, g_mix, w_in, w_pool, pool_scale, w_out, g_ffn2, w_gate2, w_up2, w_down2, g_final):
    batch, seq, _ = x_prompt.shape
    dec_batch, dec_seq, _ = x_sample.shape
    depth = w_in.shape[0]
    keep = min(WIN_MAX, seq)
    slopes = _alibi_slopes(N_HEADS)
    tm = 512
    seg = 1 + POOL_STATE + dec_seq

    xp = x_prompt.reshape(batch * seq, D_MODEL)
    xs = x_sample.reshape(dec_batch * dec_seq, D_MODEL)
    bf = lambda a: a.astype(BF16)
    row = lambda a: a.reshape(1, -1)
    kp, vp, pp, ksl, vsl, psl = [], [], [], [], [], []
    for i in range(depth):
        wg1, wu1, wd1 = bf(w_gate1[i]), bf(w_up1[i]), bf(w_down1[i])
        wg2, wu2, wd2 = bf(w_gate2[i]), bf(w_up2[i]), bf(w_down2[i])
        wi, wo = bf(w_in[i]), bf(w_out[i])
        wpool = bf(_block_diag(w_pool[i]))
        scale = row(pool_scale[i])
        gf = row(g_final) if i == depth - 1 else None

        xp = _ffn(xp, row(g_ffn1[i]), wg1, wu1, wd1, tm=tm)
        u, q, k, v, kf, vf = _inproj(xp, row(g_mix[i]), wi, tm=tm)
        u3 = u.reshape(batch, seq, POOL_WIDTH)
        y_pool = _pool_prompt(u3, wpool, scale, tp=tm).reshape(batch * seq, POOL_WIDTH)
        state = None
        for bi, (_, dil) in enumerate(BRANCHES):
            state = _attn_prompt(q, k, v, state, batch=batch, dil=dil, slopes=slopes,
                                 last=bi == len(BRANCHES) - 1)
        xp = _outproj(xp, y_pool, state, wo, tm=tm)
        xp = _ffn(xp, row(g_ffn2[i]), wg2, wu2, wd2, gf, tm=tm)
        kp.append(kf.reshape(batch, seq, N_HEADS, HEAD_DIM)[:, -keep:])
        vp.append(vf.reshape(batch, seq, N_HEADS, HEAD_DIM)[:, -keep:])
        pp.append(u3[:, -POOL_STATE:])

        xs = _ffn(xs, row(g_ffn1[i]), wg1, wu1, wd1, tm=tm)
        u, q, k, v, kf, vf = _inproj(xs, row(g_mix[i]), wi, tm=tm)
        u3 = u.reshape(dec_batch, dec_seq, POOL_WIDTH)
        ext = jnp.concatenate(
            [jnp.zeros((dec_batch, 1, POOL_WIDTH), F32), state_pool[i], u3], axis=1)
        ext_flat = jnp.pad(ext.reshape(dec_batch * seg, POOL_WIDTH), ((0, POOL_HALO), (0, 0)))
        y_pool = _pool_sample(ext_flat, wpool, scale, pos0=PAST_LEN)
        y_pool = y_pool.reshape(dec_batch, seg, POOL_WIDTH)[:, :dec_seq]
        y_pool = y_pool.reshape(dec_batch * dec_seq, POOL_WIDTH)
        y_att = _attn_sample(q, kf, vf, cache_k, cache_v, i, slopes=slopes)
        xs = _outproj(xs, y_pool, y_att, wo, tm=tm)
        xs = _ffn(xs, row(g_ffn2[i]), wg2, wu2, wd2, gf, tm=tm)
        ksl.append(kf.reshape(dec_batch, dec_seq, N_HEADS, HEAD_DIM))
        vsl.append(vf.reshape(dec_batch, dec_seq, N_HEADS, HEAD_DIM))
        psl.append(ext[:, -POOL_STATE:])

    y_prompt = xp.reshape(batch, seq, D_MODEL)
    y_sample = xs.reshape(dec_batch, dec_seq, D_MODEL)
    return (y_prompt, y_sample, jnp.stack(kp), jnp.stack(vp), jnp.stack(pp),
            jnp.stack(ksl), jnp.stack(vsl), jnp.stack(psl))
```

```python
import functools
import math

import jax
import jax.numpy as jnp
import numpy as np
from jax import lax
from jax.experimental import pallas as pl
from jax.experimental.pallas import tpu as pltpu

F32 = jnp.float32
BF16 = jnp.bfloat16

D_MODEL = 1024
D_FF = 2816
POOL_WIDTH = 256
POOL_WINDOWS = (2, 4, 8, 16)
POOL_GROUP = POOL_WIDTH // len(POOL_WINDOWS)
POOL_STATE = max(POOL_WINDOWS) - 1
POOL_HALO = 16
HEAD_DIM = 64
ATTN_WIDTH = D_MODEL - POOL_WIDTH
N_HEADS = ATTN_WIDTH // HEAD_DIM
BRANCHES = ((128, 1), (512, 4), (2048, 16))
WIN_MAX = max(w for w, _ in BRANCHES)
PAST_LEN = 2048
BLK = 128
RMS_EPS = 1e-6
NEG = -1e30
LANES = 128
SUBLANES = 8
RES = 16
FF_CHUNK = 256
VMEM_LIMIT = 56 * 1024 * 1024

assert BLK == RES * SUBLANES and all(RES % d == 0 for _, d in BRANCHES)
assert all(w // d == BLK for w, d in BRANCHES)


def _alibi_slopes(n):
    def pow2(m):
        start = 2.0 ** (-8.0 / m)
        return [start ** (i + 1) for i in range(m)]
    if math.log2(n).is_integer():
        s = pow2(n)
    else:
        c = 2 ** int(math.floor(math.log2(n)))
        s = pow2(c) + pow2(2 * c)[0::2][: n - c]
    return np.array(s, dtype=np.float32)


def _params(n_axes):
    return pltpu.CompilerParams(
        dimension_semantics=("arbitrary",) * n_axes,
        vmem_limit_bytes=VMEM_LIMIT)


def _resident(shape):
    nd = len(shape)
    return pl.BlockSpec(shape, lambda *_: (0,) * nd,
                        pipeline_mode=pl.Buffered(1))


def _layer_of(stacked, layer):
    _, k, n = stacked.shape
    return pl.BlockSpec((None, k, n), lambda *_: (layer, 0, 0),
                        pipeline_mode=pl.Buffered(1))


def _rms(x, g):
    var = jnp.mean(x * x, axis=-1, keepdims=True)
    return x * lax.rsqrt(var + RMS_EPS) * g


def _to_granules(x, batch):
    w = x.shape[-1]
    x = x.reshape(batch, -1, SUBLANES, RES, w)
    return jnp.swapaxes(x, 2, 3).reshape(-1, w)


def _from_granules(x, batch):
    w = x.shape[-1]
    x = x.reshape(batch, -1, RES, SUBLANES, w)
    return jnp.swapaxes(x, 2, 3).reshape(-1, w)


def _ffn_kernel(x_ref, g_ref, wg_ref, wu_ref, wd_ref, *rest, final):
    if final:
        gf_ref, o_ref, a_ref = rest
    else:
        o_ref, a_ref = rest
    x = x_ref[...]
    h = _rms(x, g_ref[...]).astype(BF16)
    for c in range(D_FF // FF_CHUNK):
        sl = slice(c * FF_CHUNK, (c + 1) * FF_CHUNK)
        gate = jnp.dot(h, wg_ref[:, sl], preferred_element_type=F32)
        up = jnp.dot(h, wu_ref[:, sl], preferred_element_type=F32)
        a_ref[:, sl] = (gate * jax.nn.sigmoid(gate) * up).astype(BF16)
    y = jnp.dot(a_ref[...], wd_ref[...], preferred_element_type=F32)
    xn = x + 0.5 * y
    if final:
        xn = _rms(xn, gf_ref[...])
    o_ref[...] = xn


def _ffn(x, g, wg, wu, wd, layer, g_final=None, *, tm):
    m = x.shape[0]
    final = g_final is not None
    row = pl.BlockSpec((tm, D_MODEL), lambda i: (i, 0))
    in_specs = [row, _resident((1, D_MODEL)), _layer_of(wg, layer),
                _layer_of(wu, layer), _layer_of(wd, layer)]
    args = [x, g, wg, wu, wd]
    if final:
        in_specs.append(_resident((1, D_MODEL)))
        args.append(g_final)
    return pl.pallas_call(
        functools.partial(_ffn_kernel, final=final),
        grid=(m // tm,),
        in_specs=in_specs,
        out_specs=row,
        out_shape=jax.ShapeDtypeStruct((m, D_MODEL), F32),
        scratch_shapes=[pltpu.VMEM((tm, D_FF), BF16)],
        compiler_params=_params(1),
        name="ffn_final" if final else "ffn",
    )(*args)


def _inproj_kernel(x_ref, g_ref, w_ref, u_ref, q_ref, k_ref, v_ref):
    h = _rms(x_ref[...], g_ref[...]).astype(BF16)
    a = POOL_WIDTH
    u_ref[...] = jnp.dot(h, w_ref[:, :a], preferred_element_type=F32)
    q = jnp.dot(h, w_ref[:, a:a + ATTN_WIDTH], preferred_element_type=F32)
    q_ref[...] = q * HEAD_DIM ** -0.5
    k_ref[...] = jnp.dot(h, w_ref[:, a + ATTN_WIDTH:a + 2 * ATTN_WIDTH],
                         preferred_element_type=F32)
    v_ref[...] = jnp.dot(h, w_ref[:, a + 2 * ATTN_WIDTH:], preferred_element_type=F32)


def _inproj(x, g, w, layer, *, tm):
    m = x.shape[0]
    def row(n):
        return pl.BlockSpec((tm, n), lambda i: (i, 0))
    widths = (POOL_WIDTH, ATTN_WIDTH, ATTN_WIDTH, ATTN_WIDTH)
    return pl.pallas_call(
        _inproj_kernel,
        grid=(m // tm,),
        in_specs=[row(D_MODEL), _resident((1, D_MODEL)), _layer_of(w, layer)],
        out_specs=[row(n) for n in widths],
        out_shape=[jax.ShapeDtypeStruct((m, n), F32) for n in widths],
        compiler_params=_params(1),
        name="inproj",
    )(x, g, w)


def _window_means(taps, cnt_of):
    lane = lax.broadcasted_iota(jnp.int32, (1, POOL_WIDTH), 1)
    acc = taps[0]
    mean = jnp.zeros_like(acc)
    for d in range(1, max(POOL_WINDOWS)):
        acc = acc + taps[d]
        w = d + 1
        if w in POOL_WINDOWS:
            gi = POOL_WINDOWS.index(w)
            in_group = (lane >= gi * POOL_GROUP) & (lane < (gi + 1) * POOL_GROUP)
            mean = jnp.where(in_group, acc / cnt_of(w), mean)
    return mean


def _pool_prompt_kernel(u_ref, prev_ref, w_ref, scale_ref, y_ref, *, tp):
    t = pl.program_id(1)
    sub = lax.broadcasted_iota(jnp.int32, (SUBLANES, 1), 0)
    for blk in range(tp // BLK):
        base = blk * BLK
        cur = [u_ref[0, pl.ds(base + r * SUBLANES, SUBLANES), :] for r in range(RES)]
        if blk == 0:
            prev = [jnp.where(t == 0, 0.0, prev_ref[0, pl.ds(r * SUBLANES, SUBLANES), :])
                    for r in range(RES)]
        else:
            prev = [u_ref[0, pl.ds(base - BLK + r * SUBLANES, SUBLANES), :]
                    for r in range(RES)]
        wrapped = [jnp.where(sub == 0, pltpu.roll(prev[r], 1, 0), pltpu.roll(cur[r], 1, 0))
                   for r in range(RES)]
        pos0 = (t * (tp // BLK) + blk) * BLK
        diffs = []
        for r in range(RES):
            taps = [cur[r - d] if r >= d else wrapped[r - d + RES]
                    for d in range(max(POOL_WINDOWS))]
            pos = pos0 + RES * sub + r
            mean = _window_means(taps, lambda w: jnp.minimum(w, pos + 1).astype(F32))
            diffs.append(mean - cur[r])
        diff = jnp.concatenate(diffs, axis=0).astype(BF16)
        y = jnp.dot(diff, w_ref[...], preferred_element_type=F32) * scale_ref[...]
        y_ref[0, pl.ds(base, BLK), :] = y.astype(BF16)


def _pool_prompt(u, w_bd, scale, *, tp):
    b, t, _ = u.shape
    nb = tp // BLK
    return pl.pallas_call(
        functools.partial(_pool_prompt_kernel, tp=tp),
        grid=(b, t // tp),
        in_specs=[pl.BlockSpec((1, tp, POOL_WIDTH), lambda bi, ti: (bi, ti, 0)),
                  pl.BlockSpec((1, BLK, POOL_WIDTH),
                               lambda bi, ti: (bi, jnp.maximum(ti * nb - 1, 0), 0)),
                  _resident((POOL_WIDTH, POOL_WIDTH)), _resident((1, POOL_WIDTH))],
        out_specs=pl.BlockSpec((1, tp, POOL_WIDTH), lambda bi, ti: (bi, ti, 0)),
        out_shape=jax.ShapeDtypeStruct((b, t, POOL_WIDTH), BF16),
        compiler_params=_params(2),
        name="pool_prompt",
    )(u, u, w_bd, scale)


def _pool_sample_kernel(ext_ref, w_ref, scale_ref, y_ref, *, rows, pos0):
    taps = [ext_ref[pl.ds(POOL_HALO - d, rows), :] for d in range(max(POOL_WINDOWS))]
    mean = _window_means(taps, lambda w: float(min(w, pos0 + 1)))
    diff = (mean - taps[0]).astype(BF16)
    y = jnp.dot(diff, w_ref[...], preferred_element_type=F32) * scale_ref[...]
    y_ref[...] = y.astype(BF16)


def _pool_sample(ext, w_bd, scale, *, pos0):
    assert pos0 >= POOL_STATE
    rows = ext.shape[0] - POOL_HALO
    return pl.pallas_call(
        functools.partial(_pool_sample_kernel, rows=rows, pos0=pos0),
        grid=(1,),
        in_specs=[_resident(ext.shape), _resident((POOL_WIDTH, POOL_WIDTH)),
                  _resident((1, POOL_WIDTH))],
        out_specs=pl.BlockSpec((rows, POOL_WIDTH), lambda i: (0, 0)),
        out_shape=jax.ShapeDtypeStruct((rows, POOL_WIDTH), BF16),
        compiler_params=_params(1),
        name="pool_sample",
    )(ext, w_bd, scale)


def _expand(a, e_ref):
    hi = a.astype(BF16)
    lo = (a - hi.astype(F32)).astype(BF16)
    return (jnp.dot(hi, e_ref[...], preferred_element_type=F32)
            + jnp.dot(lo, e_ref[...], preferred_element_type=F32))


def _attn_prompt_kernel(*refs, first, last):
    q_ref, kp_ref, kc_ref, vp_ref, vc_ref, tab_ref = refs[:6]
    refs = refs[6:]
    if not first:
        e_ref, acc_in, m_in, l_in = refs[:4]
        refs = refs[4:]
    if last:
        o_ref, cur_ref = refs
    else:
        acc_out, m_out, l_out, cur_ref = refs
    blk = lambda ref: ref[...].reshape(BLK, ref.shape[-1])
    j = pl.program_id(2)
    tab0 = jnp.where(j == 0, N_HEADS, 0)
    lane = lax.broadcasted_iota(jnp.int32, (1, LANES), 1)
    low = lane < HEAD_DIM
    q = blk(q_ref).astype(BF16)
    k = jnp.concatenate([blk(kp_ref), blk(kc_ref)], axis=0).astype(BF16)
    v = jnp.concatenate([blk(vp_ref), blk(vc_ref)], axis=0).astype(BF16)
    m_cur = jnp.zeros((BLK, LANES), F32)
    l_cur = jnp.ones((BLK, LANES), F32)
    for hp in range(N_HEADS // 2):
        sl = slice(hp * LANES, (hp + 1) * LANES)
        q2, k2, v2 = q[:, sl], k[:, sl], v[:, sl]
        outs = []
        for sub in range(2):
            h = 2 * hp + sub
            qh = jnp.where(low if sub == 0 else ~low, q2, jnp.zeros_like(q2))
            s = lax.dot_general(qh, k2, (((1,), (1,)), ((), ())),
                                preferred_element_type=F32)
            s = s + tab_ref[tab0 + h]
            m_h = jnp.max(s, axis=-1, keepdims=True)
            p = jnp.exp(s - m_h)
            l_h = jnp.sum(p, axis=-1, keepdims=True)
            outs.append(jnp.dot(p.astype(BF16), v2, preferred_element_type=F32))
            m_cur = jnp.where(lane == h, m_h, m_cur)
            l_cur = jnp.where(lane == h, l_h, l_cur)
        cur_ref[:, sl] = jnp.where(low, outs[0], outs[1])
    if first:
        acc, m_new, l_new = cur_ref[...], m_cur, l_cur
    else:
        m_prev = blk(m_in)
        m_new = jnp.maximum(m_prev, m_cur)
        a_prev = jnp.exp(m_prev - m_new)
        a_cur = jnp.exp(m_cur - m_new)
        l_new = a_prev * blk(l_in) + a_cur * l_cur
        acc = _expand(a_prev, e_ref) * blk(acc_in) + _expand(a_cur, e_ref) * cur_ref[...]
    if last:
        o_ref[...] = (acc * _expand(1.0 / l_new, e_ref)).reshape(o_ref.shape)
    else:
        acc_out[...] = acc.reshape(acc_out.shape)
        m_out[...] = m_new.reshape(m_out.shape)
        l_out[...] = l_new.reshape(l_out.shape)


def _branch_tables(dil, slopes):
    rho = np.arange(BLK)
    if dil == 1:
        idx = RES * (rho % SUBLANES) + rho // SUBLANES
    elif dil == 4:
        idx = 4 * (SUBLANES * (rho // 32) + rho % SUBLANES) + (rho // SUBLANES) % 4
    else:
        assert dil == RES
        idx = rho
    key = np.concatenate([idx - BLK, idx])
    delta = idx[:, None] - key[None, :]
    valid = (delta >= 0) & (delta <= BLK)
    bias = -slopes[:, None, None] * (delta * dil).astype(np.float32)[None]
    tab = np.where(valid[None], bias, np.float32(NEG)).astype(np.float32)
    tab_first = np.where((key >= 0)[None, None, :], tab, np.float32(NEG))
    return np.concatenate([tab, tab_first], axis=0)


def _head_expander():
    e = np.zeros((LANES, ATTN_WIDTH), np.float32)
    for h in range(N_HEADS):
        e[h, h * HEAD_DIM:(h + 1) * HEAD_DIM] = 1.0
    return jnp.asarray(e, dtype=BF16)


def _attn_prompt(q, k, v, state, *, batch, dil, slopes, last):
    n_rows = q.shape[0]
    t = n_rows // batch
    nb = t // dil // BLK
    first = state is None
    g = RES // dil
    def view(a):
        return a.reshape(batch, nb, dil, g, dil, SUBLANES, a.shape[-1])
    def spec(w, back):
        return pl.BlockSpec((1, 1, dil, g, 1, SUBLANES, w),
                            lambda b, r, j: (b, jnp.maximum(j - back, 0), 0, 0, r, 0, 0))
    table = jnp.asarray(_branch_tables(dil, slopes))
    in_specs = [spec(ATTN_WIDTH, 0), spec(ATTN_WIDTH, 1), spec(ATTN_WIDTH, 0),
                spec(ATTN_WIDTH, 1), spec(ATTN_WIDTH, 0), _resident(table.shape)]
    args = [view(q), view(k), view(k), view(v), view(v), table]
    if not first:
        in_specs += [_resident((LANES, ATTN_WIDTH)),
                     spec(ATTN_WIDTH, 0), spec(LANES, 0), spec(LANES, 0)]
        args += [_head_expander(), view(state[0]), view(state[1]), view(state[2])]
    shp = lambda w: jax.ShapeDtypeStruct((batch, nb, dil, g, dil, SUBLANES, w), F32)
    if last:
        out_specs, out_shape = spec(ATTN_WIDTH, 0), shp(ATTN_WIDTH)
    else:
        out_specs = [spec(ATTN_WIDTH, 0), spec(LANES, 0), spec(LANES, 0)]
        out_shape = [shp(ATTN_WIDTH), shp(LANES), shp(LANES)]
    out = pl.pallas_call(
        functools.partial(_attn_prompt_kernel, first=first, last=last),
        grid=(batch, dil, nb),
        in_specs=in_specs,
        out_specs=out_specs,
        out_shape=out_shape,
        scratch_shapes=[pltpu.VMEM((BLK, ATTN_WIDTH), F32)],
        compiler_params=_params(3),
        name=f"attn_prompt_d{dil}",
    )(*args)
    if last:
        return out.reshape(n_rows, ATTN_WIDTH)
    return (out[0].reshape(n_rows, ATTN_WIDTH), out[1].reshape(n_rows, LANES),
            out[2].reshape(n_rows, LANES))


def _attn_sample_kernel(q_ref, kn_ref, vn_ref, ck_ref, cv_ref, tabc_ref, tabn_ref,
                        o_ref, *, s_new):
    rows = N_HEADS * s_new
    q = q_ref[0]
    qt = jnp.concatenate([q] * N_HEADS, axis=0)
    r_head = lax.broadcasted_iota(jnp.int32, (rows, ATTN_WIDTH), 0) // s_new
    c_head = lax.broadcasted_iota(jnp.int32, (rows, ATTN_WIDTH), 1) // HEAD_DIM
    own = r_head == c_head
    qm32 = jnp.where(own, qt, 0.0)
    qm = qm32.astype(BF16)
    nt = (((1,), (1,)), ((), ()))
    kc = ck_ref[0, 0].astype(BF16)
    s_c = jnp.dot(qm, kc, preferred_element_type=F32) + tabc_ref[...]
    s_n = lax.dot_general(qm32, kn_ref[0], nt, preferred_element_type=F32) + tabn_ref[...]
    m = jnp.maximum(jnp.max(s_c, axis=-1, keepdims=True),
                    jnp.max(s_n, axis=-1, keepdims=True))
    p_c = jnp.exp(s_c - m)
    p_n = jnp.exp(s_n - m)
    l = jnp.sum(p_c, axis=-1, keepdims=True) + jnp.sum(p_n, axis=-1, keepdims=True)
    o = lax.dot_general(p_c.astype(BF16), cv_ref[0, 0].astype(BF16), nt,
                        preferred_element_type=F32)
    o = o + jnp.dot(p_n, vn_ref[0], preferred_element_type=F32)
    o = jnp.where(own, o / l, 0.0)
    o_ref[0] = jnp.sum(o.reshape(N_HEADS, s_new, ATTN_WIDTH), axis=0)


def _sample_tables(r_len, s_new, slopes):
    def table(dist, ok):
        cnt = np.zeros(dist.shape, np.int32)
        for w, d in BRANCHES:
            cnt += (ok & (dist % d == 0) & (dist <= w)).astype(np.int32)
        bias = (-slopes[:, None, None].astype(np.float64) * dist[None]
                + np.log(np.maximum(cnt, 1))[None])
        tab = np.where((cnt > 0)[None], bias, NEG).astype(np.float32)
        return tab.reshape(N_HEADS * s_new, dist.shape[1])
    i = np.arange(s_new)[:, None]
    dist_c = r_len + i - np.arange(r_len)[None, :]
    dist_n = i - np.arange(s_new)[None, :]
    return table(dist_c, dist_c >= 0), table(dist_n, dist_n >= 0)


def _attn_sample(q, kf, vf, cache_k, cache_v, layer, *, slopes):
    depth, b, r_len = cache_k.shape[:3]
    s_new = q.shape[0] // b
    tab_c, tab_n = _sample_tables(r_len, s_new, slopes)
    ck = jnp.transpose(cache_k, (0, 1, 3, 4, 2)).reshape(depth, b, ATTN_WIDTH, r_len)
    cv = jnp.transpose(cache_v, (0, 1, 3, 4, 2)).reshape(depth, b, ATTN_WIDTH, r_len)
    new = pl.BlockSpec((1, s_new, ATTN_WIDTH), lambda i: (i, 0, 0))
    cache = pl.BlockSpec((1, 1, ATTN_WIDTH, r_len), lambda i: (layer, i, 0, 0))
    out = pl.pallas_call(
        functools.partial(_attn_sample_kernel, s_new=s_new),
        grid=(b,),
        in_specs=[new, new, new, cache, cache,
                  _resident(tab_c.shape), _resident(tab_n.shape)],
        out_specs=new,
        out_shape=jax.ShapeDtypeStruct((b, s_new, ATTN_WIDTH), F32),
        compiler_params=_params(1),
        name="attn_sample",
    )(q.reshape(b, s_new, ATTN_WIDTH), kf.reshape(b, s_new, ATTN_WIDTH),
      vf.reshape(b, s_new, ATTN_WIDTH), ck, cv, jnp.asarray(tab_c), jnp.asarray(tab_n))
    return out.reshape(b * s_new, ATTN_WIDTH)


def _outproj_kernel(x_ref, yp_ref, ya_ref, w_ref, o_ref):
    y = jnp.dot(yp_ref[...], w_ref[:POOL_WIDTH, :], preferred_element_type=F32)
    y = y + jnp.dot(ya_ref[...].astype(BF16), w_ref[POOL_WIDTH:, :],
                    preferred_element_type=F32)
    o_ref[...] = x_ref[...] + y


def _outproj(x, yp, ya, w, layer, *, tm):
    m = x.shape[0]
    def row(n):
        return pl.BlockSpec((tm, n), lambda i: (i, 0))
    return pl.pallas_call(
        _outproj_kernel,
        grid=(m // tm,),
        in_specs=[row(D_MODEL), row(POOL_WIDTH), row(ATTN_WIDTH),
                  _layer_of(w, layer)],
        out_specs=row(D_MODEL),
        out_shape=jax.ShapeDtypeStruct((m, D_MODEL), F32),
        compiler_params=_params(1),
        name="outproj",
    )(x, yp, ya, w)


def _block_diag(w_pool):
    g, c, _ = w_pool.shape
    out = jnp.zeros((g * c, g * c), w_pool.dtype)
    for gi in range(g):
        out = lax.dynamic_update_slice(out, w_pool[gi], (gi * c, gi * c))
    return out


def kernel(x_prompt, x_sample, cache_k, cache_v, state_pool, g_ffn1, w_gate1, w_up1, w_down1, g_mix, w_in, w_pool, pool_scale, w_out, g_ffn2, w_gate2, w_up2, w_down2, g_final):
    batch, seq, _ = x_prompt.shape
    dec_batch, dec_seq, _ = x_sample.shape
    depth = w_in.shape[0]
    keep = min(WIN_MAX, seq)
    slopes = _alibi_slopes(N_HEADS)
    tm = 512
    seg = 1 + POOL_STATE + dec_seq

    xp = _to_granules(x_prompt.reshape(batch * seq, D_MODEL), batch)
    xs = x_sample.reshape(dec_batch * dec_seq, D_MODEL)
    bf = lambda a: a.astype(BF16)
    row = lambda a: a.reshape(1, -1)
    in_order = lambda a: _from_granules(a, batch).reshape(batch, seq, -1)
    kp, vp, pp, ksl, vsl, psl = [], [], [], [], [], []
    wg1, wu1, wd1 = bf(w_gate1), bf(w_up1), bf(w_down1)
    wg2, wu2, wd2 = bf(w_gate2), bf(w_up2), bf(w_down2)
    wi, wo = bf(w_in), bf(w_out)
    for i in range(depth):
        wpool = bf(_block_diag(w_pool[i]))
        scale = row(pool_scale[i])
        gf = row(g_final) if i == depth - 1 else None

        xp = _ffn(xp, row(g_ffn1[i]), wg1, wu1, wd1, i, tm=tm)
        u, q, k, v = _inproj(xp, row(g_mix[i]), wi, i, tm=tm)
        y_pool = _pool_prompt(u.reshape(batch, seq, POOL_WIDTH), wpool, scale, tp=tm)
        state = None
        for bi, (_, dil) in enumerate(BRANCHES):
            state = _attn_prompt(q, k, v, state, batch=batch, dil=dil, slopes=slopes,
                                 last=bi == len(BRANCHES) - 1)
        xp = _outproj(xp, y_pool.reshape(batch * seq, POOL_WIDTH), state, wo, i, tm=tm)
        xp = _ffn(xp, row(g_ffn2[i]), wg2, wu2, wd2, i, gf, tm=tm)
        kp.append(in_order(k)[:, -keep:].reshape(batch, keep, N_HEADS, HEAD_DIM))
        vp.append(in_order(v)[:, -keep:].reshape(batch, keep, N_HEADS, HEAD_DIM))
        u_tail = u.reshape(batch, seq, POOL_WIDTH)[:, -BLK:].reshape(batch * BLK, POOL_WIDTH)
        pp.append(_from_granules(u_tail, batch).reshape(batch, BLK, POOL_WIDTH)[:, -POOL_STATE:])

        xs = _ffn(xs, row(g_ffn1[i]), wg1, wu1, wd1, i, tm=tm)
        u, q, k, v = _inproj(xs, row(g_mix[i]), wi, i, tm=tm)
        u3 = u.reshape(dec_batch, dec_seq, POOL_WIDTH)
        ext = jnp.concatenate(
            [jnp.zeros((dec_batch, 1, POOL_WIDTH), F32), state_pool[i], u3], axis=1)
        ext_flat = jnp.pad(ext.reshape(dec_batch * seg, POOL_WIDTH), ((0, POOL_HALO), (0, 0)))
        y_pool = _pool_sample(ext_flat, wpool, scale, pos0=PAST_LEN)
        y_pool = y_pool.reshape(dec_batch, seg, POOL_WIDTH)[:, :dec_seq]
        y_pool = y_pool.reshape(dec_batch * dec_seq, POOL_WIDTH)
        y_att = _attn_sample(q, k, v, cache_k, cache_v, i, slopes=slopes)
        xs = _outproj(xs, y_pool, y_att, wo, i, tm=tm)
        xs = _ffn(xs, row(g_ffn2[i]), wg2, wu2, wd2, i, gf, tm=tm)
        ksl.append(k.reshape(dec_batch, dec_seq, N_HEADS, HEAD_DIM))
        vsl.append(v.reshape(dec_batch, dec_seq, N_HEADS, HEAD_DIM))
        psl.append(ext[:, -POOL_STATE:])

    y_prompt = in_order(xp)
    y_sample = xs.reshape(dec_batch, dec_seq, D_MODEL)
    return (y_prompt, y_sample, jnp.stack(kp), jnp.stack(vp), jnp.stack(pp),
            jnp.stack(ksl), jnp.stack(vsl), jnp.stack(psl))
```

```python
import functools
import math

import jax
import jax.numpy as jnp
import numpy as np
from jax import lax
from jax.experimental import pallas as pl
from jax.experimental.pallas import tpu as pltpu

F32 = jnp.float32
BF16 = jnp.bfloat16

D_MODEL = 1024
D_FF = 2816
POOL_WIDTH = 256
POOL_WINDOWS = (2, 4, 8, 16)
POOL_GROUP = POOL_WIDTH // len(POOL_WINDOWS)
POOL_STATE = max(POOL_WINDOWS) - 1
POOL_HALO = 16
HEAD_DIM = 64
ATTN_WIDTH = D_MODEL - POOL_WIDTH
N_HEADS = ATTN_WIDTH // HEAD_DIM
BRANCHES = ((128, 1), (512, 4), (2048, 16))
WIN_MAX = max(w for w, _ in BRANCHES)
PAST_LEN = 2048
BLK = 128
RMS_EPS = 1e-6
NEG = -1e30
LANES = 128
SUBLANES = 8
RES = 16
FF_CHUNK = 256
Q_BLOCKS_PER_STEP = 4
VMEM_LIMIT = 56 * 1024 * 1024

assert BLK == RES * SUBLANES and all(RES % d == 0 for _, d in BRANCHES)
assert all(w // d == BLK for w, d in BRANCHES)


def _alibi_slopes(n):
    def pow2(m):
        start = 2.0 ** (-8.0 / m)
        return [start ** (i + 1) for i in range(m)]
    if math.log2(n).is_integer():
        s = pow2(n)
    else:
        c = 2 ** int(math.floor(math.log2(n)))
        s = pow2(c) + pow2(2 * c)[0::2][: n - c]
    return np.array(s, dtype=np.float32)


def _params(n_axes):
    return pltpu.CompilerParams(
        dimension_semantics=("arbitrary",) * n_axes,
        vmem_limit_bytes=VMEM_LIMIT)


def _resident(shape):
    nd = len(shape)
    return pl.BlockSpec(shape, lambda *_: (0,) * nd,
                        pipeline_mode=pl.Buffered(1))


def _layer_of(stacked, layer):
    _, k, n = stacked.shape
    return pl.BlockSpec((None, k, n), lambda *_: (layer, 0, 0),
                        pipeline_mode=pl.Buffered(1))


def _rms(x, g):
    var = jnp.mean(x * x, axis=-1, keepdims=True)
    return x * lax.rsqrt(var + RMS_EPS) * g


def _to_granules(x, batch):
    w = x.shape[-1]
    x = x.reshape(batch, -1, SUBLANES, RES, w)
    return jnp.swapaxes(x, 2, 3).reshape(-1, w)


def _from_granules(x, batch):
    w = x.shape[-1]
    x = x.reshape(batch, -1, RES, SUBLANES, w)
    return jnp.swapaxes(x, 2, 3).reshape(-1, w)


def _ffn_kernel(x_ref, g_ref, wg_ref, wu_ref, wd_ref, *rest, final):
    if final:
        gf_ref, o_ref, a_ref = rest
    else:
        o_ref, a_ref = rest
    x = x_ref[...]
    h = _rms(x, g_ref[...]).astype(BF16)
    for c in range(D_FF // FF_CHUNK):
        sl = slice(c * FF_CHUNK, (c + 1) * FF_CHUNK)
        gate = jnp.dot(h, wg_ref[:, sl], preferred_element_type=F32)
        up = jnp.dot(h, wu_ref[:, sl], preferred_element_type=F32)
        a_ref[:, sl] = (gate * jax.nn.sigmoid(gate) * up).astype(BF16)
    y = jnp.dot(a_ref[...], wd_ref[...], preferred_element_type=F32)
    xn = x + 0.5 * y
    if final:
        xn = _rms(xn, gf_ref[...])
    o_ref[...] = xn


def _ffn(x, g, wg, wu, wd, layer, g_final=None, *, tm):
    m = x.shape[0]
    final = g_final is not None
    row = pl.BlockSpec((tm, D_MODEL), lambda i: (i, 0))
    in_specs = [row, _resident((1, D_MODEL)), _layer_of(wg, layer),
                _layer_of(wu, layer), _layer_of(wd, layer)]
    args = [x, g, wg, wu, wd]
    if final:
        in_specs.append(_resident((1, D_MODEL)))
        args.append(g_final)
    return pl.pallas_call(
        functools.partial(_ffn_kernel, final=final),
        grid=(m // tm,),
        in_specs=in_specs,
        out_specs=row,
        out_shape=jax.ShapeDtypeStruct((m, D_MODEL), F32),
        scratch_shapes=[pltpu.VMEM((tm, D_FF), BF16)],
        compiler_params=_params(1),
        name="ffn_final" if final else "ffn",
    )(*args)


def _inproj_kernel(x_ref, g_ref, w_ref, u_ref, q_ref, kv_ref, kt_ref, vt_ref):
    h = _rms(x_ref[...], g_ref[...]).astype(BF16)
    a = POOL_WIDTH
    u_ref[...] = jnp.dot(h, w_ref[:, :a], preferred_element_type=F32)
    q = jnp.dot(h, w_ref[:, a:a + ATTN_WIDTH], preferred_element_type=F32)
    q_ref[...] = q * HEAD_DIM ** -0.5
    k = jnp.dot(h, w_ref[:, a + ATTN_WIDTH:a + 2 * ATTN_WIDTH], preferred_element_type=F32)
    kv_ref[:, :ATTN_WIDTH] = k
    kt_ref[...] = k
    v = jnp.dot(h, w_ref[:, a + 2 * ATTN_WIDTH:], preferred_element_type=F32)
    kv_ref[:, ATTN_WIDTH:] = v
    vt_ref[...] = v


def _inproj(x, g, w, layer, *, tm, seq_rows, tail_rows):
    m = x.shape[0]
    per_seq, per_tail = seq_rows // tm, tail_rows // tm
    def row(n):
        return pl.BlockSpec((tm, n), lambda i: (i, 0))
    tail = pl.BlockSpec(
        (tm, ATTN_WIDTH),
        lambda i: ((i // per_seq) * per_tail
                   + jnp.maximum(i % per_seq - (per_seq - per_tail), 0), 0))
    n_tail = m // seq_rows * tail_rows
    return pl.pallas_call(
        _inproj_kernel,
        grid=(m // tm,),
        in_specs=[row(D_MODEL), _resident((1, D_MODEL)), _layer_of(w, layer)],
        out_specs=[row(POOL_WIDTH), row(ATTN_WIDTH), row(2 * ATTN_WIDTH), tail, tail],
        out_shape=[jax.ShapeDtypeStruct((m, POOL_WIDTH), F32),
                   jax.ShapeDtypeStruct((m, ATTN_WIDTH), F32),
                   jax.ShapeDtypeStruct((m, 2 * ATTN_WIDTH), F32),
                   jax.ShapeDtypeStruct((n_tail, ATTN_WIDTH), F32),
                   jax.ShapeDtypeStruct((n_tail, ATTN_WIDTH), F32)],
        compiler_params=_params(1),
        name="inproj",
    )(x, g, w)


def _window_means(taps, cnt_of):
    lane = lax.broadcasted_iota(jnp.int32, (1, POOL_WIDTH), 1)
    acc = taps[0]
    mean = jnp.zeros_like(acc)
    for d in range(1, max(POOL_WINDOWS)):
        acc = acc + taps[d]
        w = d + 1
        if w in POOL_WINDOWS:
            gi = POOL_WINDOWS.index(w)
            in_group = (lane >= gi * POOL_GROUP) & (lane < (gi + 1) * POOL_GROUP)
            mean = jnp.where(in_group, acc / cnt_of(w), mean)
    return mean


def _pool_prompt_kernel(u_ref, prev_ref, w_ref, scale_ref, y_ref, *, tp):
    t = pl.program_id(1)
    sub = lax.broadcasted_iota(jnp.int32, (SUBLANES, 1), 0)
    for blk in range(tp // BLK):
        base = blk * BLK
        cur = [u_ref[0, pl.ds(base + r * SUBLANES, SUBLANES), :] for r in range(RES)]
        if blk == 0:
            prev = [jnp.where(t == 0, 0.0, prev_ref[0, pl.ds(r * SUBLANES, SUBLANES), :])
                    for r in range(RES)]
        else:
            prev = [u_ref[0, pl.ds(base - BLK + r * SUBLANES, SUBLANES), :]
                    for r in range(RES)]
        wrapped = [jnp.where(sub == 0, pltpu.roll(prev[r], 1, 0), pltpu.roll(cur[r], 1, 0))
                   for r in range(RES)]
        pos0 = (t * (tp // BLK) + blk) * BLK
        diffs = []
        for r in range(RES):
            taps = [cur[r - d] if r >= d else wrapped[r - d + RES]
                    for d in range(max(POOL_WINDOWS))]
            pos = pos0 + RES * sub + r
            mean = _window_means(taps, lambda w: jnp.minimum(w, pos + 1).astype(F32))
            diffs.append(mean - cur[r])
        diff = jnp.concatenate(diffs, axis=0).astype(BF16)
        y = jnp.dot(diff, w_ref[...], preferred_element_type=F32) * scale_ref[...]
        y_ref[0, pl.ds(base, BLK), :] = y.astype(BF16)


def _pool_prompt(u, w_bd, scale, *, tp):
    b, t, _ = u.shape
    nb = tp // BLK
    return pl.pallas_call(
        functools.partial(_pool_prompt_kernel, tp=tp),
        grid=(b, t // tp),
        in_specs=[pl.BlockSpec((1, tp, POOL_WIDTH), lambda bi, ti: (bi, ti, 0)),
                  pl.BlockSpec((1, BLK, POOL_WIDTH),
                               lambda bi, ti: (bi, jnp.maximum(ti * nb - 1, 0), 0)),
                  _resident((POOL_WIDTH, POOL_WIDTH)), _resident((1, POOL_WIDTH))],
        out_specs=pl.BlockSpec((1, tp, POOL_WIDTH), lambda bi, ti: (bi, ti, 0)),
        out_shape=jax.ShapeDtypeStruct((b, t, POOL_WIDTH), BF16),
        compiler_params=_params(2),
        name="pool_prompt",
    )(u, u, w_bd, scale)


def _pool_sample_kernel(ext_ref, w_ref, scale_ref, y_ref, *, rows, pos0):
    taps = [ext_ref[pl.ds(POOL_HALO - d, rows), :] for d in range(max(POOL_WINDOWS))]
    mean = _window_means(taps, lambda w: float(min(w, pos0 + 1)))
    diff = (mean - taps[0]).astype(BF16)
    y = jnp.dot(diff, w_ref[...], preferred_element_type=F32) * scale_ref[...]
    y_ref[...] = y.astype(BF16)


def _pool_sample(ext, w_bd, scale, *, pos0):
    assert pos0 >= POOL_STATE
    rows = ext.shape[0] - POOL_HALO
    return pl.pallas_call(
        functools.partial(_pool_sample_kernel, rows=rows, pos0=pos0),
        grid=(1,),
        in_specs=[_resident(ext.shape), _resident((POOL_WIDTH, POOL_WIDTH)),
                  _resident((1, POOL_WIDTH))],
        out_specs=pl.BlockSpec((rows, POOL_WIDTH), lambda i: (0, 0)),
        out_shape=jax.ShapeDtypeStruct((rows, POOL_WIDTH), BF16),
        compiler_params=_params(1),
        name="pool_sample",
    )(ext, w_bd, scale)


def _expand(a, e_ref):
    hi = a.astype(BF16)
    lo = (a - hi.astype(F32)).astype(BF16)
    return (jnp.dot(hi, e_ref[...], preferred_element_type=F32)
            + jnp.dot(lo, e_ref[...], preferred_element_type=F32))


def _attn_prompt_kernel(*refs, first, last, n_q):
    q_ref, kvp_ref, kvc_ref, tab_ref = refs[:4]
    refs = refs[4:]
    if not first:
        e_ref, acc_in, ml_in = refs[:3]
        refs = refs[3:]
    if last:
        o_ref, kv_buf, cur_ref = refs
    else:
        acc_out, ml_out, kv_buf, cur_ref = refs
    rows = lambda ref, i: ref[0, 0, i].reshape(BLK, ref.shape[-1])
    j = pl.program_id(2)
    lane = lax.broadcasted_iota(jnp.int32, (1, LANES), 1)
    low = lane < HEAD_DIM
    n_pairs = N_HEADS // 2
    kv_buf[pl.ds(0, BLK), :] = kvp_ref[...].reshape(BLK, 2 * ATTN_WIDTH).astype(BF16)
    for i in range(n_q):
        kv_buf[pl.ds((i + 1) * BLK, BLK), :] = rows(kvc_ref, i).astype(BF16)

    def one_block(i, carry):
        tab0 = jnp.where((j == 0) & (i == 0), n_pairs, 0)
        key0 = pl.multiple_of(i * BLK, BLK)
        q = rows(q_ref, i).astype(BF16)
        m_cur = jnp.zeros((BLK, LANES), F32)
        l_cur = jnp.ones((BLK, LANES), F32)
        for hp in range(n_pairs):
            sl = slice(hp * LANES, (hp + 1) * LANES)
            q2 = q[:, sl]
            zero = jnp.zeros_like(q2)
            qq = jnp.concatenate([jnp.where(low, q2, zero), jnp.where(low, zero, q2)], axis=0)
            k2 = kv_buf[pl.ds(key0, 2 * BLK), sl]
            v2 = kv_buf[pl.ds(key0, 2 * BLK), ATTN_WIDTH + hp * LANES:ATTN_WIDTH + (hp + 1) * LANES]
            s = lax.dot_general(qq, k2, (((1,), (1,)), ((), ())), preferred_element_type=F32)
            s = s + tab_ref[tab0 + hp]
            m_h = jnp.max(s, axis=-1, keepdims=True)
            p = jnp.exp(s - m_h)
            l_h = jnp.sum(p, axis=-1, keepdims=True)
            o = jnp.dot(p.astype(BF16), v2, preferred_element_type=F32)
            cur_ref[:, sl] = jnp.where(low, o[:BLK], o[BLK:])
            for sub in range(2):
                part = slice(sub * BLK, (sub + 1) * BLK)
                m_cur = jnp.where(lane == 2 * hp + sub, m_h[part], m_cur)
                l_cur = jnp.where(lane == 2 * hp + sub, l_h[part], l_cur)
        if first:
            acc, m_new, l_new = cur_ref[...], m_cur, l_cur
        else:
            ml = rows(ml_in, i)
            m_prev, l_prev = ml[:, :LANES], ml[:, LANES:]
            m_new = jnp.maximum(m_prev, m_cur)
            a_prev = jnp.exp(m_prev - m_new)
            a_cur = jnp.exp(m_cur - m_new)
            l_new = a_prev * l_prev + a_cur * l_cur
            acc = (_expand(a_prev, e_ref) * rows(acc_in, i)
                   + _expand(a_cur, e_ref) * cur_ref[...])
        if last:
            o_ref[0, 0, i] = (acc * _expand(1.0 / l_new, e_ref)).reshape(o_ref.shape[3:])
        else:
            acc_out[0, 0, i] = acc.reshape(acc_out.shape[3:])
            ml_out[0, 0, i] = jnp.concatenate([m_new, l_new], axis=1).reshape(ml_out.shape[3:])
        return carry

    lax.fori_loop(0, n_q, one_block, 0)


def _branch_tables(dil, slopes):
    rho = np.arange(BLK)
    if dil == 1:
        idx = RES * (rho % SUBLANES) + rho // SUBLANES
    elif dil == 4:
        idx = 4 * (SUBLANES * (rho // 32) + rho % SUBLANES) + (rho // SUBLANES) % 4
    else:
        assert dil == RES
        idx = rho
    key = np.concatenate([idx - BLK, idx])
    delta = idx[:, None] - key[None, :]
    valid = (delta >= 0) & (delta <= BLK)
    bias = -slopes[:, None, None] * (delta * dil).astype(np.float32)[None]
    tab = np.where(valid[None], bias, np.float32(NEG)).astype(np.float32)
    tab_first = np.where((key >= 0)[None, None, :], tab, np.float32(NEG))
    pairs = lambda a: a.reshape(N_HEADS // 2, 2 * BLK, 2 * BLK)
    return np.concatenate([pairs(tab), pairs(tab_first)], axis=0)


def _head_expander():
    e = np.zeros((LANES, ATTN_WIDTH), np.float32)
    for h in range(N_HEADS):
        e[h, h * HEAD_DIM:(h + 1) * HEAD_DIM] = 1.0
    return jnp.asarray(e, dtype=BF16)


def _attn_prompt(q, kv, state, *, batch, dil, slopes, last):
    n_rows = q.shape[0]
    t = n_rows // batch
    nb = t // dil // BLK
    n_q = min(Q_BLOCKS_PER_STEP, nb)
    first = state is None
    g = RES // dil
    def view(a):
        return a.reshape(batch, nb // n_q, n_q, dil, g, dil, SUBLANES, a.shape[-1])
    def spec(w):
        return pl.BlockSpec((1, 1, n_q, dil, g, 1, SUBLANES, w),
                            lambda b, r, j: (b, j, 0, 0, 0, r, 0, 0))
    def prev_spec(w):
        return pl.BlockSpec((1, 1, 1, dil, g, 1, SUBLANES, w),
                            lambda b, r, j: (b, jnp.maximum(j * n_q - 1, 0), 0, 0, 0, r, 0, 0))
    def prev_view(a):
        return a.reshape(batch, nb, 1, dil, g, dil, SUBLANES, a.shape[-1])
    table = jnp.asarray(_branch_tables(dil, slopes))
    in_specs = [spec(ATTN_WIDTH), prev_spec(2 * ATTN_WIDTH), spec(2 * ATTN_WIDTH),
                _resident(table.shape)]
    args = [view(q), prev_view(kv), view(kv), table]
    if not first:
        in_specs += [_resident((LANES, ATTN_WIDTH)), spec(ATTN_WIDTH), spec(2 * LANES)]
        args += [_head_expander(), view(state[0]), view(state[1])]
    shp = lambda w: jax.ShapeDtypeStruct(
        (batch, nb // n_q, n_q, dil, g, dil, SUBLANES, w), F32)
    if last:
        out_specs, out_shape = spec(ATTN_WIDTH), shp(ATTN_WIDTH)
    else:
        out_specs = [spec(ATTN_WIDTH), spec(2 * LANES)]
        out_shape = [shp(ATTN_WIDTH), shp(2 * LANES)]
    out = pl.pallas_call(
        functools.partial(_attn_prompt_kernel, first=first, last=last, n_q=n_q),
        grid=(batch, dil, nb // n_q),
        in_specs=in_specs,
        out_specs=out_specs,
        out_shape=out_shape,
        scratch_shapes=[pltpu.VMEM(((n_q + 1) * BLK, 2 * ATTN_WIDTH), BF16),
                        pltpu.VMEM((BLK, ATTN_WIDTH), F32)],
        compiler_params=_params(3),
        name=f"attn_prompt_d{dil}",
    )(*args)
    if last:
        return out.reshape(n_rows, ATTN_WIDTH)
    return out[0].reshape(n_rows, ATTN_WIDTH), out[1].reshape(n_rows, 2 * LANES)


def _attn_sample_kernel(q_ref, kn_ref, vn_ref, ck_ref, cv_ref, tabc_ref, tabn_ref,
                        o_ref, *, s_new):
    rows = N_HEADS * s_new
    q = q_ref[0]
    qt = jnp.concatenate([q] * N_HEADS, axis=0)
    r_head = lax.broadcasted_iota(jnp.int32, (rows, ATTN_WIDTH), 0) // s_new
    c_head = lax.broadcasted_iota(jnp.int32, (rows, ATTN_WIDTH), 1) // HEAD_DIM
    own = r_head == c_head
    qm32 = jnp.where(own, qt, 0.0)
    qm = qm32.astype(BF16)
    nt = (((1,), (1,)), ((), ()))
    kc = ck_ref[0, 0].astype(BF16)
    s_c = jnp.dot(qm, kc, preferred_element_type=F32) + tabc_ref[...]
    s_n = lax.dot_general(qm32, kn_ref[0], nt, preferred_element_type=F32) + tabn_ref[...]
    m = jnp.maximum(jnp.max(s_c, axis=-1, keepdims=True),
                    jnp.max(s_n, axis=-1, keepdims=True))
    p_c = jnp.exp(s_c - m)
    p_n = jnp.exp(s_n - m)
    l = jnp.sum(p_c, axis=-1, keepdims=True) + jnp.sum(p_n, axis=-1, keepdims=True)
    o = lax.dot_general(p_c.astype(BF16), cv_ref[0, 0].astype(BF16), nt,
                        preferred_element_type=F32)
    o = o + jnp.dot(p_n, vn_ref[0], preferred_element_type=F32)
    o = jnp.where(own, o / l, 0.0)
    o_ref[0] = jnp.sum(o.reshape(N_HEADS, s_new, ATTN_WIDTH), axis=0)


def _sample_tables(r_len, s_new, slopes):
    def table(dist, ok):
        cnt = np.zeros(dist.shape, np.int32)
        for w, d in BRANCHES:
            cnt += (ok & (dist % d == 0) & (dist <= w)).astype(np.int32)
        bias = (-slopes[:, None, None].astype(np.float64) * dist[None]
                + np.log(np.maximum(cnt, 1))[None])
        tab = np.where((cnt > 0)[None], bias, NEG).astype(np.float32)
        return tab.reshape(N_HEADS * s_new, dist.shape[1])
    i = np.arange(s_new)[:, None]
    dist_c = r_len + i - np.arange(r_len)[None, :]
    dist_n = i - np.arange(s_new)[None, :]
    return table(dist_c, dist_c >= 0), table(dist_n, dist_n >= 0)


def _attn_sample(q, kf, vf, cache_k, cache_v, layer, *, slopes):
    depth, b, r_len = cache_k.shape[:3]
    s_new = q.shape[0] // b
    tab_c, tab_n = _sample_tables(r_len, s_new, slopes)
    ck = jnp.transpose(cache_k, (0, 1, 3, 4, 2)).reshape(depth, b, ATTN_WIDTH, r_len)
    cv = jnp.transpose(cache_v, (0, 1, 3, 4, 2)).reshape(depth, b, ATTN_WIDTH, r_len)
    new = pl.BlockSpec((1, s_new, ATTN_WIDTH), lambda i: (i, 0, 0))
    cache = pl.BlockSpec((1, 1, ATTN_WIDTH, r_len), lambda i: (layer, i, 0, 0))
    out = pl.pallas_call(
        functools.partial(_attn_sample_kernel, s_new=s_new),
        grid=(b,),
        in_specs=[new, new, new, cache, cache,
                  _resident(tab_c.shape), _resident(tab_n.shape)],
        out_specs=new,
        out_shape=jax.ShapeDtypeStruct((b, s_new, ATTN_WIDTH), F32),
        compiler_params=_params(1),
        name="attn_sample",
    )(q.reshape(b, s_new, ATTN_WIDTH), kf.reshape(b, s_new, ATTN_WIDTH),
      vf.reshape(b, s_new, ATTN_WIDTH), ck, cv, jnp.asarray(tab_c), jnp.asarray(tab_n))
    return out.reshape(b * s_new, ATTN_WIDTH)


def _outproj_kernel(x_ref, yp_ref, ya_ref, w_ref, o_ref):
    y = jnp.dot(yp_ref[...], w_ref[:POOL_WIDTH, :], preferred_element_type=F32)
    y = y + jnp.dot(ya_ref[...].astype(BF16), w_ref[POOL_WIDTH:, :],
                    preferred_element_type=F32)
    o_ref[...] = x_ref[...] + y


def _outproj(x, yp, ya, w, layer, *, tm):
    m = x.shape[0]
    def row(n):
        return pl.BlockSpec((tm, n), lambda i: (i, 0))
    return pl.pallas_call(
        _outproj_kernel,
        grid=(m // tm,),
        in_specs=[row(D_MODEL), row(POOL_WIDTH), row(ATTN_WIDTH),
                  _layer_of(w, layer)],
        out_specs=row(D_MODEL),
        out_shape=jax.ShapeDtypeStruct((m, D_MODEL), F32),
        compiler_params=_params(1),
        name="outproj",
    )(x, yp, ya, w)


def _block_diag(w_pool):
    g, c, _ = w_pool.shape
    out = jnp.zeros((g * c, g * c), w_pool.dtype)
    for gi in range(g):
        out = lax.dynamic_update_slice(out, w_pool[gi], (gi * c, gi * c))
    return out


def kernel(x_prompt, x_sample, cache_k, cache_v, state_pool, g_ffn1, w_gate1, w_up1, w_down1, g_mix, w_in, w_pool, pool_scale, w_out, g_ffn2, w_gate2, w_up2, w_down2, g_final):
    batch, seq, _ = x_prompt.shape
    dec_batch, dec_seq, _ = x_sample.shape
    depth = w_in.shape[0]
    keep = min(WIN_MAX, seq)
    slopes = _alibi_slopes(N_HEADS)
    tm = 512
    seg =1 + POOL_STATE + dec_seq

    xp = _to_granules(x_prompt.reshape(batch * seq, D_MODEL), batch)
    xs = x_sample.reshape(dec_batch * dec_seq, D_MODEL)
    bf = lambda a: a.astype(BF16)
    row = lambda a: a.reshape(1, -1)
    in_order = lambda a: _from_granules(a, batch).reshape(batch, seq, -1)
    kp, vp, pp, ksl, vsl, psl = [], [], [], [], [], []
    wg1, wu1, wd1 = bf(w_gate1), bf(w_up1), bf(w_down1)
    wg2, wu2, wd2 = bf(w_gate2), bf(w_up2), bf(w_down2)
    wi, wo = bf(w_in), bf(w_out)
    for i in range(depth):
        wpool = bf(_block_diag(w_pool[i]))
        scale = row(pool_scale[i])
        gf = row(g_final) if i == depth - 1 else None

        xp = _ffn(xp, row(g_ffn1[i]), wg1, wu1, wd1, i, tm=tm)
        u, q, kv, k, v = _inproj(xp, row(g_mix[i]), wi, i, tm=tm, seq_rows=seq,
                                 tail_rows=keep)
        y_pool = _pool_prompt(u.reshape(batch, seq, POOL_WIDTH), wpool, scale, tp=tm)
        state = None
        for bi, (_, dil) in enumerate(BRANCHES):
            state = _attn_prompt(q, kv, state, batch=batch, dil=dil, slopes=slopes,
                                 last=bi == len(BRANCHES) - 1)
        xp = _outproj(xp, y_pool.reshape(batch * seq, POOL_WIDTH), state, wo, i, tm=tm)
        xp = _ffn(xp, row(g_ffn2[i]), wg2, wu2, wd2, i, gf, tm=tm)
        tail_order = lambda a: _from_granules(a, batch).reshape(batch, keep, N_HEADS, HEAD_DIM)
        kp.append(tail_order(k))
        vp.append(tail_order(v))
        u_tail = u.reshape(batch, seq, POOL_WIDTH)[:, -BLK:].reshape(batch * BLK, POOL_WIDTH)
        pp.append(_from_granules(u_tail, batch).reshape(batch, BLK, POOL_WIDTH)[:, -POOL_STATE:])

        xs = _ffn(xs, row(g_ffn1[i]), wg1, wu1, wd1, i, tm=tm)
        u, q, _, k, v = _inproj(xs, row(g_mix[i]), wi, i, tm=tm, seq_rows=xs.shape[0],
                                tail_rows=xs.shape[0])
        u3 = u.reshape(dec_batch, dec_seq, POOL_WIDTH)
        ext = jnp.concatenate(
            [jnp.zeros((dec_batch, 1, POOL_WIDTH), F32), state_pool[i], u3], axis=1)
        ext_flat = jnp.pad(ext.reshape(dec_batch * seg, POOL_WIDTH), ((0, POOL_HALO), (0, 0)))
        y_pool = _pool_sample(ext_flat, wpool, scale, pos0=PAST_LEN)
        y_pool = y_pool.reshape(dec_batch, seg, POOL_WIDTH)[:, :dec_seq]
        y_pool = y_pool.reshape(dec_batch * dec_seq, POOL_WIDTH)
        y_att = _attn_sample(q, k, v, cache_k, cache_v, i, slopes=slopes)
        xs = _outproj(xs, y_pool, y_att, wo, i, tm=tm)
        xs = _ffn(xs, row(g_ffn2[i]), wg2, wu2, wd2, i, gf, tm=tm)
        ksl.append(k.reshape(dec_batch, dec_seq, N_HEADS, HEAD_DIM))
        vsl.append(v.reshape(dec_batch, dec_seq, N_HEADS, HEAD_DIM))
        psl.append(ext[:, -POOL_STATE:])

    y_prompt = in_order(xp)
    y_sample = xs.reshape(dec_batch, dec_seq, D_MODEL)
    return (y_prompt, y_sample, jnp.stack(kp), jnp.stack(vp), jnp.stack(pp),
            jnp.stack(ksl), jnp.stack(vsl), jnp.stack(psl))
```

```python
import functools
import math

import jax
import jax.numpy as jnp
import numpy as np
from jax import lax
from jax.experimental import pallas as pl
from jax.experimental.pallas import tpu as pltpu

F32 = jnp.float32
BF16 = jnp.bfloat16

D_MODEL = 1024
D_FF = 2816
POOL_WIDTH = 256
POOL_WINDOWS = (2, 4, 8, 16)
POOL_GROUP = POOL_WIDTH // len(POOL_WINDOWS)
POOL_STATE = max(POOL_WINDOWS) - 1
POOL_HALO = 16
HEAD_DIM = 64
ATTN_WIDTH = D_MODEL - POOL_WIDTH
N_HEADS = ATTN_WIDTH // HEAD_DIM
BRANCHES = ((128, 1), (512, 4), (2048, 16))
WIN_MAX = max(w for w, _ in BRANCHES)
PAST_LEN = 2048
BLK = 128
RMS_EPS = 1e-6
NEG = -1e30
LANES = 128
SUBLANES = 8
RES = 16
FF_CHUNK = 256
Q_BLOCKS_PER_STEP = 4
VMEM_LIMIT = 56 * 1024 * 1024

assert BLK == RES * SUBLANES and all(RES % d == 0 for _, d in BRANCHES)
assert all(w // d == BLK for w, d in BRANCHES)


def _alibi_slopes(n):
    def pow2(m):
        start = 2.0 ** (-8.0 / m)
        return [start ** (i + 1) for i in range(m)]
    if math.log2(n).is_integer():
        s = pow2(n)
    else:
        c = 2 ** int(math.floor(math.log2(n)))
        s = pow2(c) + pow2(2 * c)[0::2][: n - c]
    return np.array(s, dtype=np.float32)


def _params(n_axes):
    return pltpu.CompilerParams(
        dimension_semantics=("arbitrary",) * n_axes,
        vmem_limit_bytes=VMEM_LIMIT)


def _resident(shape):
    nd = len(shape)
    return pl.BlockSpec(shape, lambda *_: (0,) * nd,
                        pipeline_mode=pl.Buffered(1))


def _layer_of(stacked, layer):
    _, k, n = stacked.shape
    return pl.BlockSpec((None, k, n), lambda *_: (layer, 0, 0),
                        pipeline_mode=pl.Buffered(1))


def _rms(x, g):
    var = jnp.mean(x * x, axis=-1, keepdims=True)
    return x * lax.rsqrt(var + RMS_EPS) * g


def _to_granules(x, batch):
    w = x.shape[-1]
    x = x.reshape(batch, -1, SUBLANES, RES, w)
    return jnp.swapaxes(x, 2, 3).reshape(-1, w)


def _from_granules(x, batch):
    w = x.shape[-1]
    x = x.reshape(batch, -1, RES, SUBLANES, w)
    return jnp.swapaxes(x, 2, 3).reshape(-1, w)


def _ffn_kernel(x_ref, g_ref, wg_ref, wu_ref, wd_ref, *rest, final):
    if final:
        gf_ref, o_ref, a_ref = rest
    else:
        o_ref, a_ref = rest
    x = x_ref[...]
    h = _rms(x, g_ref[...]).astype(BF16)
    for c in range(D_FF // FF_CHUNK):
        sl = slice(c * FF_CHUNK, (c + 1) * FF_CHUNK)
        gate = jnp.dot(h, wg_ref[:, sl], preferred_element_type=F32)
        up = jnp.dot(h, wu_ref[:, sl], preferred_element_type=F32)
        a_ref[:, sl] = (gate * jax.nn.sigmoid(gate) * up).astype(BF16)
    y = jnp.dot(a_ref[...], wd_ref[...], preferred_element_type=F32)
    xn = x + 0.5 * y
    if final:
        xn = _rms(xn, gf_ref[...])
    o_ref[...] = xn


def _ffn(x, g, wg, wu, wd, layer, g_final=None, *, tm):
    m = x.shape[0]
    final = g_final is not None
    row = pl.BlockSpec((tm, D_MODEL), lambda i: (i, 0))
    in_specs = [row, _resident((1, D_MODEL)), _layer_of(wg, layer),
                _layer_of(wu, layer), _layer_of(wd, layer)]
    args = [x, g, wg, wu, wd]
    if final:
        in_specs.append(_resident((1, D_MODEL)))
        args.append(g_final)
    return pl.pallas_call(
        functools.partial(_ffn_kernel, final=final),
        grid=(m // tm,),
        in_specs=in_specs,
        out_specs=row,
        out_shape=jax.ShapeDtypeStruct((m, D_MODEL), F32),
        scratch_shapes=[pltpu.VMEM((tm, D_FF), BF16)],
        compiler_params=_params(1),
        name="ffn_final" if final else "ffn",
    )(*args)


def _store_positions_on_lanes(val, out_ref, slab_ref):
    tm = val.shape[0]
    n_slabs = ATTN_WIDTH // LANES
    for c in range(n_slabs):
        slab_ref[c] = val[:, c * LANES:(c + 1) * LANES]
    for c in range(n_slabs):
        for b in range(tm // BLK):
            in_order = jnp.concatenate(
                [slab_ref[c, pl.ds(b * BLK + jj, RES, stride=SUBLANES), :]
                 for jj in range(SUBLANES)], axis=0)
            out_ref[c * LANES:(c + 1) * LANES, b * BLK:(b + 1) * BLK] = in_order.T


def _inproj_kernel(x_ref, g_ref, w_ref, *rest, returned_layout):
    if returned_layout:
        u_ref, q_ref, kv_ref, kt_ref, vt_ref, slab_ref = rest[-6:]
    else:
        u_ref, q_ref, kv_ref, kt_ref, vt_ref = rest
    h = _rms(x_ref[...], g_ref[...]).astype(BF16)
    a = POOL_WIDTH
    u_ref[...] = jnp.dot(h, w_ref[:, :a], preferred_element_type=F32)
    q = jnp.dot(h, w_ref[:, a:a + ATTN_WIDTH], preferred_element_type=F32)
    q_ref[...] = q * HEAD_DIM ** -0.5
    k = jnp.dot(h, w_ref[:, a + ATTN_WIDTH:a + 2 * ATTN_WIDTH], preferred_element_type=F32)
    kv_ref[:, :ATTN_WIDTH] = k
    v = jnp.dot(h, w_ref[:, a + 2 * ATTN_WIDTH:], preferred_element_type=F32)
    kv_ref[:, ATTN_WIDTH:] = v
    if returned_layout:
        _store_positions_on_lanes(k, kt_ref, slab_ref)
        _store_positions_on_lanes(v, vt_ref, slab_ref)
    else:
        kt_ref[...] = k
        vt_ref[...] = v


def _inproj(x, g, w, layer, *, tm, seq_rows, tail_rows, returned=None):
    m = x.shape[0]
    n_seq = m // seq_rows
    per_seq, per_tail = seq_rows // tm, tail_rows // tm
    def row(n):
        return pl.BlockSpec((tm, n), lambda i: (i, 0))
    tail_tile = lambda i: jnp.maximum(i % per_seq - (per_seq - per_tail), 0)
    in_specs = [row(D_MODEL), _resident((1, D_MODEL)), _layer_of(w, layer)]
    args = [x, g, w]
    aliases, scratch = {}, []
    if returned is None:
        tail = pl.BlockSpec((tm, ATTN_WIDTH),
                            lambda i: ((i // per_seq) * per_tail + tail_tile(i), 0))
        tail_shape = jax.ShapeDtypeStruct((n_seq * tail_rows, ATTN_WIDTH), F32)
    else:
        depth = w.shape[0]
        tail = pl.BlockSpec((None, None, ATTN_WIDTH, tm),
                            lambda i: (layer, i // per_seq, 0, tail_tile(i)))
        tail_shape = jax.ShapeDtypeStruct((depth, n_seq, ATTN_WIDTH, tail_rows), F32)
        scratch = [pltpu.VMEM((ATTN_WIDTH // LANES, tm, LANES), F32)]
        if returned:
            in_specs += [pl.BlockSpec(memory_space=pl.ANY)] * 2
            args += list(returned)
            aliases = {3: 3, 4: 4}
    return pl.pallas_call(
        functools.partial(_inproj_kernel, returned_layout=returned is not None),
        grid=(m // tm,),
        in_specs=in_specs,
        out_specs=[row(POOL_WIDTH), row(ATTN_WIDTH), row(2 * ATTN_WIDTH), tail, tail],
        out_shape=[jax.ShapeDtypeStruct((m, POOL_WIDTH), F32),
                   jax.ShapeDtypeStruct((m, ATTN_WIDTH), F32),
                   jax.ShapeDtypeStruct((m, 2 * ATTN_WIDTH), F32),
                   tail_shape, tail_shape],
        scratch_shapes=scratch,
        input_output_aliases=aliases,
        compiler_params=_params(1),
        name="inproj",
    )(*args)


def _window_means(taps, cnt_of):
    lane = lax.broadcasted_iota(jnp.int32, (1, POOL_WIDTH), 1)
    acc = taps[0]
    mean = jnp.zeros_like(acc)
    for d in range(1, max(POOL_WINDOWS)):
        acc = acc + taps[d]
        w = d + 1
        if w in POOL_WINDOWS:
            gi = POOL_WINDOWS.index(w)
            in_group = (lane >= gi * POOL_GROUP) & (lane < (gi + 1) * POOL_GROUP)
            mean = jnp.where(in_group, acc / cnt_of(w), mean)
    return mean


def _pool_prompt_kernel(u_ref, prev_ref, w_ref, scale_ref, y_ref, *, tp):
    t = pl.program_id(1)
    sub = lax.broadcasted_iota(jnp.int32, (SUBLANES, 1), 0)
    for blk in range(tp // BLK):
        base = blk * BLK
        cur = [u_ref[0, pl.ds(base + r * SUBLANES, SUBLANES), :] for r in range(RES)]
        if blk == 0:
            prev = [jnp.where(t == 0, 0.0, prev_ref[0, pl.ds(r * SUBLANES, SUBLANES), :])
                    for r in range(RES)]
        else:
            prev = [u_ref[0, pl.ds(base - BLK + r * SUBLANES, SUBLANES), :]
                    for r in range(RES)]
        wrapped = [jnp.where(sub == 0, pltpu.roll(prev[r], 1, 0), pltpu.roll(cur[r], 1, 0))
                   for r in range(RES)]
        pos0 = (t * (tp // BLK) + blk) * BLK
        diffs = []
        for r in range(RES):
            taps = [cur[r - d] if r >= d else wrapped[r - d + RES]
                    for d in range(max(POOL_WINDOWS))]
            pos = pos0 + RES * sub + r
            mean = _window_means(taps, lambda w: jnp.minimum(w, pos + 1).astype(F32))
            diffs.append(mean - cur[r])
        diff = jnp.concatenate(diffs, axis=0).astype(BF16)
        y = jnp.dot(diff, w_ref[...], preferred_element_type=F32) * scale_ref[...]
        y_ref[0, pl.ds(base, BLK), :] = y.astype(BF16)


def _pool_prompt(u, w_bd, scale, *, tp):
    b, t, _ = u.shape
    nb = tp // BLK
    return pl.pallas_call(
        functools.partial(_pool_prompt_kernel, tp=tp),
        grid=(b, t // tp),
        in_specs=[pl.BlockSpec((1, tp, POOL_WIDTH), lambda bi, ti: (bi, ti, 0)),
                  pl.BlockSpec((1, BLK, POOL_WIDTH),
                               lambda bi, ti: (bi, jnp.maximum(ti * nb - 1, 0), 0)),
                  _resident((POOL_WIDTH, POOL_WIDTH)), _resident((1, POOL_WIDTH))],
        out_specs=pl.BlockSpec((1, tp, POOL_WIDTH), lambda bi, ti: (bi, ti, 0)),
        out_shape=jax.ShapeDtypeStruct((b, t, POOL_WIDTH), BF16),
        compiler_params=_params(2),
        name="pool_prompt",
    )(u, u, w_bd, scale)


def _pool_sample_kernel(ext_ref, w_ref, scale_ref, y_ref, *, rows, pos0):
    taps = [ext_ref[pl.ds(POOL_HALO - d, rows), :] for d in range(max(POOL_WINDOWS))]
    mean = _window_means(taps, lambda w: float(min(w, pos0 + 1)))
    diff = (mean - taps[0]).astype(BF16)
    y = jnp.dot(diff, w_ref[...], preferred_element_type=F32) * scale_ref[...]
    y_ref[...] = y.astype(BF16)


def _pool_sample(ext, w_bd, scale, *, pos0):
    assert pos0 >= POOL_STATE
    rows = ext.shape[0] - POOL_HALO
    return pl.pallas_call(
        functools.partial(_pool_sample_kernel, rows=rows, pos0=pos0),
        grid=(1,),
        in_specs=[_resident(ext.shape), _resident((POOL_WIDTH, POOL_WIDTH)),
                  _resident((1, POOL_WIDTH))],
        out_specs=pl.BlockSpec((rows, POOL_WIDTH), lambda i: (0, 0)),
        out_shape=jax.ShapeDtypeStruct((rows, POOL_WIDTH), BF16),
        compiler_params=_params(1),
        name="pool_sample",
    )(ext, w_bd, scale)


def _expand(a, e_ref):
    hi = a.astype(BF16)
    lo = (a - hi.astype(F32)).astype(BF16)
    return (jnp.dot(hi, e_ref[...], preferred_element_type=F32)
            + jnp.dot(lo, e_ref[...], preferred_element_type=F32))


def _attn_prompt_kernel(*refs, first, last, n_q):
    q_ref, kvp_ref, kvc_ref, tab_ref = refs[:4]
    refs = refs[4:]
    if not first:
        e_ref, acc_in, ml_in = refs[:3]
        refs = refs[3:]
    if last:
        o_ref, kv_buf, cur_ref = refs
    else:
        acc_out, ml_out, kv_buf, cur_ref = refs
    rows = lambda ref, i: ref[0, 0, i].reshape(BLK, ref.shape[-1])
    j = pl.program_id(2)
    lane = lax.broadcasted_iota(jnp.int32, (1, LANES), 1)
    low = lane < HEAD_DIM
    n_pairs = N_HEADS // 2
    kv_buf[pl.ds(0, BLK), :] = kvp_ref[...].reshape(BLK, 2 * ATTN_WIDTH).astype(BF16)
    for i in range(n_q):
        kv_buf[pl.ds((i + 1) * BLK, BLK), :] = rows(kvc_ref, i).astype(BF16)

    def one_block(i, carry):
        tab0 = jnp.where((j == 0) & (i == 0), n_pairs, 0)
        key0 = pl.multiple_of(i * BLK, BLK)
        q = rows(q_ref, i).astype(BF16)
        m_cur = jnp.zeros((BLK, LANES), F32)
        l_cur = jnp.ones((BLK, LANES), F32)
        for hp in range(n_pairs):
            sl = slice(hp * LANES, (hp + 1) * LANES)
            q2 = q[:, sl]
            zero = jnp.zeros_like(q2)
            qq = jnp.concatenate([jnp.where(low, q2, zero), jnp.where(low, zero, q2)], axis=0)
            k2 = kv_buf[pl.ds(key0, 2 * BLK), sl]
            v2 = kv_buf[pl.ds(key0, 2 * BLK), ATTN_WIDTH + hp * LANES:ATTN_WIDTH + (hp + 1) * LANES]
            s = lax.dot_general(qq, k2, (((1,), (1,)), ((), ())), preferred_element_type=F32)
            s = s + tab_ref[tab0 + hp]
            m_h = jnp.max(s, axis=-1, keepdims=True)
            p = jnp.exp(s - m_h)
            l_h = jnp.sum(p, axis=-1, keepdims=True)
            o = jnp.dot(p.astype(BF16), v2, preferred_element_type=F32)
            cur_ref[:, sl] = jnp.where(low, o[:BLK], o[BLK:])
            for sub in range(2):
                part = slice(sub * BLK, (sub + 1) * BLK)
                m_cur = jnp.where(lane == 2 * hp + sub, m_h[part], m_cur)
                l_cur = jnp.where(lane == 2 * hp + sub, l_h[part], l_cur)
        if first:
            acc, m_new, l_new = cur_ref[...], m_cur, l_cur
        else:
            ml = rows(ml_in, i)
            m_prev, l_prev = ml[:, :LANES], ml[:, LANES:]
            m_new = jnp.maximum(m_prev, m_cur)
            a_prev = jnp.exp(m_prev - m_new)
            a_cur = jnp.exp(m_cur - m_new)
            l_new = a_prev * l_prev + a_cur * l_cur
            acc = (_expand(a_prev, e_ref) * rows(acc_in, i)
                   + _expand(a_cur, e_ref) * cur_ref[...])
        if last:
            o_ref[0, 0, i] = (acc * _expand(1.0 / l_new, e_ref)).reshape(o_ref.shape[3:])
        else:
            acc_out[0, 0, i] = acc.reshape(acc_out.shape[3:])
            ml_out[0, 0, i] = jnp.concatenate([m_new, l_new], axis=1).reshape(ml_out.shape[3:])
        return carry

    lax.fori_loop(0, n_q, one_block, 0)


def _branch_tables(dil, slopes):
    rho = np.arange(BLK)
    if dil == 1:
        idx = RES * (rho % SUBLANES) + rho // SUBLANES
    elif dil == 4:
        idx = 4 * (SUBLANES * (rho // 32) + rho % SUBLANES) + (rho // SUBLANES) % 4
    else:
        assert dil == RES
        idx = rho
    key = np.concatenate([idx - BLK, idx])
    delta = idx[:, None] - key[None, :]
    valid = (delta >= 0) & (delta <= BLK)
    bias = -slopes[:, None, None] * (delta * dil).astype(np.float32)[None]
    tab = np.where(valid[None], bias, np.float32(NEG)).astype(np.float32)
    tab_first = np.where((key >= 0)[None, None, :], tab, np.float32(NEG))
    pairs = lambda a: a.reshape(N_HEADS // 2, 2 * BLK, 2 * BLK)
    return np.concatenate([pairs(tab), pairs(tab_first)], axis=0)


def _head_expander():
    e = np.zeros((LANES, ATTN_WIDTH), np.float32)
    for h in range(N_HEADS):
        e[h, h * HEAD_DIM:(h + 1) * HEAD_DIM] = 1.0
    return jnp.asarray(e, dtype=BF16)


def _attn_prompt(q, kv, state, *, batch, dil, slopes, last):
    n_rows = q.shape[0]
    t = n_rows // batch
    nb = t // dil // BLK
    n_q = min(Q_BLOCKS_PER_STEP, nb)
    first = state is None
    g = RES // dil
    def view(a):
        return a.reshape(batch, nb // n_q, n_q, dil, g, dil, SUBLANES, a.shape[-1])
    def spec(w):
        return pl.BlockSpec((1, 1, n_q, dil, g, 1, SUBLANES, w),
                            lambda b, r, j: (b, j, 0, 0, 0, r, 0, 0))
    def prev_spec(w):
        return pl.BlockSpec((1, 1, 1, dil, g, 1, SUBLANES, w),
                            lambda b, r, j: (b, jnp.maximum(j * n_q - 1, 0), 0, 0, 0, r, 0, 0))
    def prev_view(a):
        return a.reshape(batch, nb, 1, dil, g, dil, SUBLANES, a.shape[-1])
    table = jnp.asarray(_branch_tables(dil, slopes))
    in_specs = [spec(ATTN_WIDTH), prev_spec(2 * ATTN_WIDTH), spec(2 * ATTN_WIDTH),
                _resident(table.shape)]
    args = [view(q), prev_view(kv), view(kv), table]
    if not first:
        in_specs += [_resident((LANES, ATTN_WIDTH)), spec(ATTN_WIDTH), spec(2 * LANES)]
        args += [_head_expander(), view(state[0]), view(state[1])]
    shp = lambda w: jax.ShapeDtypeStruct(
        (batch, nb // n_q, n_q, dil, g, dil, SUBLANES, w), F32)
    if last:
        out_specs, out_shape = spec(ATTN_WIDTH), shp(ATTN_WIDTH)
    else:
        out_specs = [spec(ATTN_WIDTH), spec(2 * LANES)]
        out_shape = [shp(ATTN_WIDTH), shp(2 * LANES)]
    out = pl.pallas_call(
        functools.partial(_attn_prompt_kernel, first=first, last=last, n_q=n_q),
        grid=(batch, dil, nb // n_q),
        in_specs=in_specs,
        out_specs=out_specs,
        out_shape=out_shape,
        scratch_shapes=[pltpu.VMEM(((n_q + 1) * BLK, 2 * ATTN_WIDTH), BF16),
                        pltpu.VMEM((BLK, ATTN_WIDTH), F32)],
        compiler_params=_params(3),
        name=f"attn_prompt_d{dil}",
    )(*args)
    if last:
        return out.reshape(n_rows, ATTN_WIDTH)
    return out[0].reshape(n_rows, ATTN_WIDTH), out[1].reshape(n_rows, 2 * LANES)


def _attn_sample_kernel(q_ref, kn_ref, vn_ref, ck_ref, cv_ref, tabc_ref, tabn_ref,
                        o_ref, *, s_new):
    rows = N_HEADS * s_new
    q = q_ref[0]
    qt = jnp.concatenate([q] * N_HEADS, axis=0)
    r_head = lax.broadcasted_iota(jnp.int32, (rows, ATTN_WIDTH), 0) // s_new
    c_head = lax.broadcasted_iota(jnp.int32, (rows, ATTN_WIDTH), 1) // HEAD_DIM
    own = r_head == c_head
    qm32 = jnp.where(own, qt, 0.0)
    qm = qm32.astype(BF16)
    nt = (((1,), (1,)), ((), ()))
    kc = ck_ref[0, 0].astype(BF16)
    s_c = jnp.dot(qm, kc, preferred_element_type=F32) + tabc_ref[...]
    s_n = lax.dot_general(qm32, kn_ref[0], nt, preferred_element_type=F32) + tabn_ref[...]
    m = jnp.maximum(jnp.max(s_c, axis=-1, keepdims=True),
                    jnp.max(s_n, axis=-1, keepdims=True))
    p_c = jnp.exp(s_c - m)
    p_n = jnp.exp(s_n - m)
    l = jnp.sum(p_c, axis=-1, keepdims=True) + jnp.sum(p_n, axis=-1, keepdims=True)
    o = lax.dot_general(p_c.astype(BF16), cv_ref[0, 0].astype(BF16), nt,
                        preferred_element_type=F32)
    o = o + jnp.dot(p_n, vn_ref[0], preferred_element_type=F32)
    o = jnp.where(own, o / l, 0.0)
    o_ref[0] = jnp.sum(o.reshape(N_HEADS, s_new, ATTN_WIDTH), axis=0)


def _sample_tables(r_len, s_new, slopes):
    def table(dist, ok):
        cnt = np.zeros(dist.shape, np.int32)
        for w, d in BRANCHES:
            cnt += (ok & (dist % d == 0) & (dist <= w)).astype(np.int32)
        bias = (-slopes[:, None, None].astype(np.float64) * dist[None]
                + np.log(np.maximum(cnt, 1))[None])
        tab = np.where((cnt > 0)[None], bias, NEG).astype(np.float32)
        return tab.reshape(N_HEADS * s_new, dist.shape[1])
    i = np.arange(s_new)[:, None]
    dist_c = r_len + i - np.arange(r_len)[None, :]
    dist_n = i - np.arange(s_new)[None, :]
    return table(dist_c, dist_c >= 0), table(dist_n, dist_n >= 0)


def _attn_sample(q, kf, vf, cache_k, cache_v, layer, *, slopes):
    depth, b, r_len = cache_k.shape[:3]
    s_new = q.shape[0] // b
    tab_c, tab_n = _sample_tables(r_len, s_new, slopes)
    ck = jnp.transpose(cache_k, (0, 1, 3, 4, 2)).reshape(depth, b, ATTN_WIDTH, r_len)
    cv = jnp.transpose(cache_v, (0, 1, 3, 4, 2)).reshape(depth, b, ATTN_WIDTH, r_len)
    new = pl.BlockSpec((1, s_new, ATTN_WIDTH), lambda i: (i, 0, 0))
    cache = pl.BlockSpec((1, 1, ATTN_WIDTH, r_len), lambda i: (layer, i, 0, 0))
    out = pl.pallas_call(
        functools.partial(_attn_sample_kernel, s_new=s_new),
        grid=(b,),
        in_specs=[new, new, new, cache, cache,
                  _resident(tab_c.shape), _resident(tab_n.shape)],
        out_specs=new,
        out_shape=jax.ShapeDtypeStruct((b, s_new, ATTN_WIDTH), F32),
        compiler_params=_params(1),
        name="attn_sample",
    )(q.reshape(b, s_new, ATTN_WIDTH), kf.reshape(b, s_new, ATTN_WIDTH),
      vf.reshape(b, s_new, ATTN_WIDTH), ck, cv, jnp.asarray(tab_c), jnp.asarray(tab_n))
    return out.reshape(b * s_new, ATTN_WIDTH)


def _outproj_kernel(x_ref, yp_ref, ya_ref, w_ref, o_ref):
    y = jnp.dot(yp_ref[...], w_ref[:POOL_WIDTH, :], preferred_element_type=F32)
    y = y + jnp.dot(ya_ref[...].astype(BF16), w_ref[POOL_WIDTH:, :],
                    preferred_element_type=F32)
    o_ref[...] = x_ref[...] + y


def _outproj(x, yp, ya, w, layer, *, tm):
    m = x.shape[0]
    def row(n):
        return pl.BlockSpec((tm, n), lambda i: (i, 0))
    return pl.pallas_call(
        _outproj_kernel,
        grid=(m // tm,),
        in_specs=[row(D_MODEL), row(POOL_WIDTH), row(ATTN_WIDTH),
                  _layer_of(w, layer)],
        out_specs=row(D_MODEL),
        out_shape=jax.ShapeDtypeStruct((m, D_MODEL), F32),
        compiler_params=_params(1),
        name="outproj",
    )(x, yp, ya, w)


def _block_diag(w_pool):
    g, c, _ = w_pool.shape
    out = jnp.zeros((g * c, g * c), w_pool.dtype)
    for gi in range(g):
        out = lax.dynamic_update_slice(out, w_pool[gi], (gi * c, gi * c))
    return out


def kernel(x_prompt, x_sample, cache_k, cache_v, state_pool, g_ffn1, w_gate1, w_up1, w_down1, g_mix, w_in, w_pool, pool_scale, w_out, g_ffn2, w_gate2, w_up2, w_down2, g_final):
    batch, seq, _ = x_prompt.shape
    dec_batch, dec_seq, _ = x_sample.shape
    depth = w_in.shape[0]
    keep = min(WIN_MAX, seq)
    slopes = _alibi_slopes(N_HEADS)
    tm = 512
    seg =1 + POOL_STATE + dec_seq

    xp = _to_granules(x_prompt.reshape(batch * seq, D_MODEL), batch)
    xs = x_sample.reshape(dec_batch * dec_seq, D_MODEL)
    bf = lambda a: a.astype(BF16)
    row = lambda a: a.reshape(1, -1)
    in_order = lambda a: _from_granules(a, batch).reshape(batch, seq, -1)
    kv_prompt = ()
    pp, ksl, vsl, psl = [], [], [], []
    wg1, wu1, wd1 = bf(w_gate1), bf(w_up1), bf(w_down1)
    wg2, wu2, wd2 = bf(w_gate2), bf(w_up2), bf(w_down2)
    wi, wo = bf(w_in), bf(w_out)
    for i in range(depth):
        wpool = bf(_block_diag(w_pool[i]))
        scale = row(pool_scale[i])
        gf = row(g_final) if i == depth - 1 else None

        xp = _ffn(xp, row(g_ffn1[i]), wg1, wu1, wd1, i, tm=tm)
        u, q, kv, *kv_prompt = _inproj(xp, row(g_mix[i]), wi, i, tm=tm, seq_rows=seq,
                                       tail_rows=keep, returned=kv_prompt)
        y_pool = _pool_prompt(u.reshape(batch, seq, POOL_WIDTH), wpool, scale, tp=tm)
        state = None
        for bi, (_, dil) in enumerate(BRANCHES):
            state = _attn_prompt(q, kv, state, batch=batch, dil=dil, slopes=slopes,
                                 last=bi == len(BRANCHES) - 1)
        xp = _outproj(xp, y_pool.reshape(batch * seq, POOL_WIDTH), state, wo, i, tm=tm)
        xp = _ffn(xp, row(g_ffn2[i]), wg2, wu2, wd2, i, gf, tm=tm)
        u_tail = u.reshape(batch, seq, POOL_WIDTH)[:, -BLK:].reshape(batch * BLK, POOL_WIDTH)
        pp.append(_from_granules(u_tail, batch).reshape(batch, BLK, POOL_WIDTH)[:, -POOL_STATE:])

        xs = _ffn(xs, row(g_ffn1[i]), wg1, wu1, wd1, i, tm=tm)
        u, q, _, k, v = _inproj(xs, row(g_mix[i]), wi, i, tm=tm, seq_rows=xs.shape[0],
                                tail_rows=xs.shape[0])
        u3 = u.reshape(dec_batch, dec_seq, POOL_WIDTH)
        ext = jnp.concatenate(
            [jnp.zeros((dec_batch, 1, POOL_WIDTH), F32), state_pool[i], u3], axis=1)
        ext_flat = jnp.pad(ext.reshape(dec_batch * seg, POOL_WIDTH), ((0, POOL_HALO), (0, 0)))
        y_pool = _pool_sample(ext_flat, wpool, scale, pos0=PAST_LEN)
        y_pool = y_pool.reshape(dec_batch, seg, POOL_WIDTH)[:, :dec_seq]
        y_pool = y_pool.reshape(dec_batch * dec_seq, POOL_WIDTH)
        y_att = _attn_sample(q, k, v, cache_k, cache_v, i, slopes=slopes)
        xs = _outproj(xs, y_pool, y_att, wo, i, tm=tm)
        xs = _ffn(xs, row(g_ffn2[i]), wg2, wu2, wd2, i, gf, tm=tm)
        ksl.append(k.reshape(dec_batch, dec_seq, N_HEADS, HEAD_DIM))
        vsl.append(v.reshape(dec_batch, dec_seq, N_HEADS, HEAD_DIM))
        psl.append(ext[:, -POOL_STATE:])

    y_prompt = in_order(xp)
    y_sample = xs.reshape(dec_batch, dec_seq, D_MODEL)
    kp, vp = (jnp.transpose(a.reshape(depth, batch, N_HEADS, HEAD_DIM, keep), (0, 1, 4, 2, 3))
              for a in kv_prompt)
    return (y_prompt, y_sample, kp, vp, jnp.stack(pp),
            jnp.stack(ksl), jnp.stack(vsl), jnp.stack(psl))
```

```python
import functools
import math

import jax
import jax.numpy as jnp
import numpy as np
from jax import lax
from jax.experimental import pallas as pl
from jax.experimental.pallas import tpu as pltpu

F32 = jnp.float32
BF16 = jnp.bfloat16

D_MODEL = 1024
D_FF = 2816
POOL_WIDTH = 256
POOL_WINDOWS = (2, 4, 8, 16)
POOL_GROUP = POOL_WIDTH // len(POOL_WINDOWS)
POOL_STATE = max(POOL_WINDOWS) - 1
POOL_HALO = 16
HEAD_DIM = 64
ATTN_WIDTH = D_MODEL - POOL_WIDTH
N_HEADS = ATTN_WIDTH // HEAD_DIM
BRANCHES = ((128, 1), (512, 4), (2048, 16))
WIN_MAX = max(w for w, _ in BRANCHES)
PAST_LEN = 2048
BLK = 128
RMS_EPS = 1e-6
NEG = -1e30
LANES = 128
SUBLANES = 8
RES = 16
FF_CHUNK = 256
Q_BLOCKS_PER_STEP = 8
VMEM_LIMIT = 56 * 1024 * 1024

assert BLK == RES * SUBLANES and all(RES % d == 0 for _, d in BRANCHES)
assert all(w // d == BLK for w, d in BRANCHES)


def _alibi_slopes(n):
    def pow2(m):
        start = 2.0 ** (-8.0 / m)
        return [start ** (i + 1) for i in range(m)]
    if math.log2(n).is_integer():
        s = pow2(n)
    else:
        c = 2 ** int(math.floor(math.log2(n)))
        s = pow2(c) + pow2(2 * c)[0::2][: n - c]
    return np.array(s, dtype=np.float32)


def _params(n_axes):
    return pltpu.CompilerParams(
        dimension_semantics=("arbitrary",) * n_axes,
        vmem_limit_bytes=VMEM_LIMIT)


def _resident(shape):
    nd = len(shape)
    return pl.BlockSpec(shape, lambda *_: (0,) * nd,
                        pipeline_mode=pl.Buffered(1))


def _layer_of(stacked, layer):
    _, k, n = stacked.shape
    return pl.BlockSpec((None, k, n), lambda *_: (layer, 0, 0),
                        pipeline_mode=pl.Buffered(1))


def _rms(x, g):
    var = jnp.mean(x * x, axis=-1, keepdims=True)
    return x * lax.rsqrt(var + RMS_EPS) * g


def _to_granules(x, batch):
    w = x.shape[-1]
    x = x.reshape(batch, -1, SUBLANES, RES, w)
    return jnp.swapaxes(x, 2, 3).reshape(-1, w)


def _from_granules(x, batch):
    w = x.shape[-1]
    x = x.reshape(batch, -1, RES, SUBLANES, w)
    return jnp.swapaxes(x, 2, 3).reshape(-1, w)


def _ffn_kernel(x_ref, g_ref, wg_ref, wu_ref, wd_ref, *rest, mixer, final, hosted):
    rest = list(rest)
    if mixer:
        yp_ref, ya_ref, wo_ref = rest[:3]
        rest = rest[3:]
    if final:
        gf_ref = rest.pop(0)
    if hosted:
        attn_in, rest = rest[:7], rest[7:]
        o_ref, os_ref, a_ref = rest
    else:
        o_ref, a_ref = rest
    x = x_ref[...]
    if mixer:
        x = x + jnp.dot(yp_ref[...], wo_ref[:POOL_WIDTH, :], preferred_element_type=F32)
        x = x + jnp.dot(ya_ref[...].astype(BF16), wo_ref[POOL_WIDTH:, :],
                        preferred_element_type=F32)
    h = _rms(x, g_ref[...]).astype(BF16)
    for c in range(D_FF // FF_CHUNK):
        sl = slice(c * FF_CHUNK, (c + 1) * FF_CHUNK)
        gate = jnp.dot(h, wg_ref[:, sl], preferred_element_type=F32)
        up = jnp.dot(h, wu_ref[:, sl], preferred_element_type=F32)
        a_ref[:, sl] = (gate * jax.nn.sigmoid(gate) * up).astype(BF16)
    y = jnp.dot(a_ref[...], wd_ref[...], preferred_element_type=F32)
    xn = x + 0.5 * y
    if final:
        xn = _rms(xn, gf_ref[...])
    o_ref[...] = xn
    if hosted:
        os_ref[0] = _attn_sample_body(*attn_in)


def _ffn(x, g, wg, wu, wd, layer, *, tm, mixer=None, g_final=None, hosted=None):
    m = x.shape[0]
    row = lambda n: pl.BlockSpec((tm, n), lambda i: (i, 0))
    in_specs = [row(D_MODEL), _resident((1, D_MODEL)), _layer_of(wg, layer),
                _layer_of(wu, layer), _layer_of(wd, layer)]
    args = [x, g, wg, wu, wd]
    if mixer is not None:
        in_specs += [row(POOL_WIDTH), row(ATTN_WIDTH), _layer_of(mixer[2], layer)]
        args += list(mixer)
    if g_final is not None:
        in_specs.append(_resident((1, D_MODEL)))
        args.append(g_final)
    out_specs = row(D_MODEL)
    out_shape = jax.ShapeDtypeStruct((m, D_MODEL), F32)
    if hosted is not None:
        in_specs += hosted[0]
        args += hosted[1]
        out_specs, out_shape = [out_specs, hosted[2]], [out_shape, hosted[3]]
    return pl.pallas_call(
        functools.partial(_ffn_kernel, mixer=mixer is not None,
                          final=g_final is not None, hosted=hosted is not None),
        grid=(m // tm,),
        in_specs=in_specs,
        out_specs=out_specs,
        out_shape=out_shape,
        scratch_shapes=[pltpu.VMEM((tm, D_FF), BF16)],
        compiler_params=_params(1),
        name="ffn" + ("_mix" if mixer is not None else "")
             + ("_final" if g_final is not None else "") + ("_attn" if hosted is not None else ""),
    )(*args)


def _store_positions_on_lanes(val, out_ref, slab_ref):
    tm = val.shape[0]
    n_slabs = ATTN_WIDTH // LANES
    for c in range(n_slabs):
        slab_ref[c] = val[:, c * LANES:(c + 1) * LANES]
    for c in range(n_slabs):
        for b in range(tm // BLK):
            in_order = jnp.concatenate(
                [slab_ref[c, pl.ds(b * BLK + jj, RES, stride=SUBLANES), :]
                 for jj in range(SUBLANES)], axis=0)
            out_ref[c * LANES:(c + 1) * LANES, b * BLK:(b + 1) * BLK] = in_order.T


def _inproj_kernel(x_ref, g_ref, w_ref, *rest, returned_layout):
    if returned_layout:
        u_ref, q_ref, kv_ref, kt_ref, vt_ref, slab_ref = rest[-6:]
    else:
        u_ref, q_ref, kv_ref, kt_ref, vt_ref = rest
    h = _rms(x_ref[...], g_ref[...]).astype(BF16)
    a = POOL_WIDTH
    u_ref[...] = jnp.dot(h, w_ref[:, :a], preferred_element_type=F32)
    q = jnp.dot(h, w_ref[:, a:a + ATTN_WIDTH], preferred_element_type=F32)
    q_ref[...] = q * HEAD_DIM ** -0.5
    k = jnp.dot(h, w_ref[:, a + ATTN_WIDTH:a + 2 * ATTN_WIDTH], preferred_element_type=F32)
    kv_ref[:, :ATTN_WIDTH] = k
    v = jnp.dot(h, w_ref[:, a + 2 * ATTN_WIDTH:], preferred_element_type=F32)
    kv_ref[:, ATTN_WIDTH:] = v
    if returned_layout:
        _store_positions_on_lanes(k, kt_ref, slab_ref)
        _store_positions_on_lanes(v, vt_ref, slab_ref)
    else:
        kt_ref[...] = k
        vt_ref[...] = v


def _inproj(x, g, w, layer, *, tm, seq_rows, tail_rows, returned=None):
    m = x.shape[0]
    n_seq = m // seq_rows
    per_seq, per_tail = seq_rows // tm, tail_rows // tm
    def row(n):
        return pl.BlockSpec((tm, n), lambda i: (i, 0))
    tail_tile = lambda i: jnp.maximum(i % per_seq - (per_seq - per_tail), 0)
    in_specs = [row(D_MODEL), _resident((1, D_MODEL)), _layer_of(w, layer)]
    args = [x, g, w]
    aliases, scratch = {}, []
    if returned is None:
        tail = pl.BlockSpec((tm, ATTN_WIDTH),
                            lambda i: ((i // per_seq) * per_tail + tail_tile(i), 0))
        tail_shape = jax.ShapeDtypeStruct((n_seq * tail_rows, ATTN_WIDTH), F32)
    else:
        depth = w.shape[0]
        tail = pl.BlockSpec((None, None, ATTN_WIDTH, tm),
                            lambda i: (layer, i // per_seq, 0, tail_tile(i)))
        tail_shape = jax.ShapeDtypeStruct((depth, n_seq, ATTN_WIDTH, tail_rows), F32)
        scratch = [pltpu.VMEM((ATTN_WIDTH // LANES, tm, LANES), F32)]
        if returned:
            in_specs += [pl.BlockSpec(memory_space=pl.ANY)] * 2
            args += list(returned)
            aliases = {3: 3, 4: 4}
    return pl.pallas_call(
        functools.partial(_inproj_kernel, returned_layout=returned is not None),
        grid=(m // tm,),
        in_specs=in_specs,
        out_specs=[row(POOL_WIDTH), row(ATTN_WIDTH), row(2 * ATTN_WIDTH), tail, tail],
        out_shape=[jax.ShapeDtypeStruct((m, POOL_WIDTH), F32),
                   jax.ShapeDtypeStruct((m, ATTN_WIDTH), F32),
                   jax.ShapeDtypeStruct((m, 2 * ATTN_WIDTH), F32),
                   tail_shape, tail_shape],
        scratch_shapes=scratch,
        input_output_aliases=aliases,
        compiler_params=_params(1),
        name="inproj",
    )(*args)


def _window_means(taps, cnt_of):
    lane = lax.broadcasted_iota(jnp.int32, (1, POOL_WIDTH), 1)
    acc = taps[0]
    mean = jnp.zeros_like(acc)
    for d in range(1, max(POOL_WINDOWS)):
        acc = acc + taps[d]
        w = d + 1
        if w in POOL_WINDOWS:
            gi = POOL_WINDOWS.index(w)
            in_group = (lane >= gi * POOL_GROUP) & (lane < (gi + 1) * POOL_GROUP)
            mean = jnp.where(in_group, acc / cnt_of(w), mean)
    return mean


def _pool_prompt_kernel(u_ref, prev_ref, w_ref, scale_ref, y_ref, *, tp):
    t = pl.program_id(1)
    sub = lax.broadcasted_iota(jnp.int32, (SUBLANES, 1), 0)
    for blk in range(tp // BLK):
        base = blk * BLK
        cur = [u_ref[0, pl.ds(base + r * SUBLANES, SUBLANES), :] for r in range(RES)]
        if blk == 0:
            prev = [jnp.where(t == 0, 0.0, prev_ref[0, pl.ds(r * SUBLANES, SUBLANES), :])
                    for r in range(RES)]
        else:
            prev = [u_ref[0, pl.ds(base - BLK + r * SUBLANES, SUBLANES), :]
                    for r in range(RES)]
        wrapped = [jnp.where(sub == 0, pltpu.roll(prev[r], 1, 0), pltpu.roll(cur[r], 1, 0))
                   for r in range(RES)]
        pos0 = (t * (tp // BLK) + blk) * BLK
        diffs = []
        for r in range(RES):
            taps = [cur[r - d] if r >= d else wrapped[r - d + RES]
                    for d in range(max(POOL_WINDOWS))]
            pos = pos0 + RES * sub + r
            mean = _window_means(taps, lambda w: jnp.minimum(w, pos + 1).astype(F32))
            diffs.append(mean - cur[r])
        diff = jnp.concatenate(diffs, axis=0).astype(BF16)
        y = jnp.dot(diff, w_ref[...], preferred_element_type=F32) * scale_ref[...]
        y_ref[0, pl.ds(base, BLK), :] = y.astype(BF16)


def _pool_prompt(u, w_bd, scale, *, tp):
    b, t, _ = u.shape
    nb = tp // BLK
    return pl.pallas_call(
        functools.partial(_pool_prompt_kernel, tp=tp),
        grid=(b, t // tp),
        in_specs=[pl.BlockSpec((1, tp, POOL_WIDTH), lambda bi, ti: (bi, ti, 0)),
                  pl.BlockSpec((1, BLK, POOL_WIDTH),
                               lambda bi, ti: (bi, jnp.maximum(ti * nb - 1, 0), 0)),
                  _resident((POOL_WIDTH, POOL_WIDTH)), _resident((1, POOL_WIDTH))],
        out_specs=pl.BlockSpec((1, tp, POOL_WIDTH), lambda bi, ti: (bi, ti, 0)),
        out_shape=jax.ShapeDtypeStruct((b, t, POOL_WIDTH), BF16),
        compiler_params=_params(2),
        name="pool_prompt",
    )(u, u, w_bd, scale)


def _pool_sample_kernel(ext_ref, w_ref, scale_ref, y_ref, *, rows, pos0):
    taps = [ext_ref[pl.ds(POOL_HALO - d, rows), :] for d in range(max(POOL_WINDOWS))]
    mean = _window_means(taps, lambda w: float(min(w, pos0 + 1)))
    diff = (mean - taps[0]).astype(BF16)
    y = jnp.dot(diff, w_ref[...], preferred_element_type=F32) * scale_ref[...]
    y_ref[...] = y.astype(BF16)


def _pool_sample(ext, w_bd, scale, *, pos0):
    assert pos0 >= POOL_STATE
    rows = ext.shape[0] - POOL_HALO
    return pl.pallas_call(
        functools.partial(_pool_sample_kernel, rows=rows, pos0=pos0),
        grid=(1,),
        in_specs=[_resident(ext.shape), _resident((POOL_WIDTH, POOL_WIDTH)),
                  _resident((1, POOL_WIDTH))],
        out_specs=pl.BlockSpec((rows, POOL_WIDTH), lambda i: (0, 0)),
        out_shape=jax.ShapeDtypeStruct((rows, POOL_WIDTH), BF16),
        compiler_params=_params(1),
        name="pool_sample",
    )(ext, w_bd, scale)


def _expand(a, e_ref):
    hi = a.astype(BF16)
    lo = (a - hi.astype(F32)).astype(BF16)
    return (jnp.dot(hi, e_ref[...], preferred_element_type=F32)
            + jnp.dot(lo, e_ref[...], preferred_element_type=F32))


def _attn_prompt_kernel(*refs, first, last, n_q):
    q_ref, kvp_ref, kvc_ref, tab_ref = refs[:4]
    refs = refs[4:]
    if not first:
        e_ref, acc_in, ml_in = refs[:3]
        refs = refs[3:]
    if last:
        o_ref, kv_buf, cur_ref = refs
    else:
        acc_out, ml_out, kv_buf, cur_ref = refs
    rows = lambda ref, i: ref[0, 0, i].reshape(BLK, ref.shape[-1])
    j = pl.program_id(2)
    lane = lax.broadcasted_iota(jnp.int32, (1, LANES), 1)
    low = lane < HEAD_DIM
    n_pairs = N_HEADS // 2
    kv_buf[pl.ds(0, BLK), :] = kvp_ref[...].reshape(BLK, 2 * ATTN_WIDTH).astype(BF16)
    for i in range(n_q):
        kv_buf[pl.ds((i + 1) * BLK, BLK), :] = rows(kvc_ref, i).astype(BF16)

    def one_block(i, carry):
        tab0 = jnp.where((j == 0) & (i == 0), n_pairs, 0)
        key0 = pl.multiple_of(i * BLK, BLK)
        q = rows(q_ref, i).astype(BF16)
        m_cur = jnp.zeros((BLK, LANES), F32)
        l_cur = jnp.ones((BLK, LANES), F32)
        for hp in range(n_pairs):
            sl = slice(hp * LANES, (hp + 1) * LANES)
            q2 = q[:, sl]
            zero = jnp.zeros_like(q2)
            qq = jnp.concatenate([jnp.where(low, q2, zero), jnp.where(low, zero, q2)], axis=0)
            k2 = kv_buf[pl.ds(key0, 2 * BLK), sl]
            v2 = kv_buf[pl.ds(key0, 2 * BLK), ATTN_WIDTH + hp * LANES:ATTN_WIDTH + (hp + 1) * LANES]
            s = lax.dot_general(qq, k2, (((1,), (1,)), ((), ())), preferred_element_type=F32)
            s = s + tab_ref[tab0 + hp]
            m_h = jnp.max(s, axis=-1, keepdims=True)
            p = jnp.exp(s - m_h)
            l_h = jnp.sum(p, axis=-1, keepdims=True)
            o = jnp.dot(p.astype(BF16), v2, preferred_element_type=F32)
            cur_ref[:, sl] = jnp.where(low, o[:BLK], o[BLK:])
            for sub in range(2):
                part = slice(sub * BLK, (sub + 1) * BLK)
                m_cur = jnp.where(lane == 2 * hp + sub, m_h[part], m_cur)
                l_cur = jnp.where(lane == 2 * hp + sub, l_h[part], l_cur)
        if first:
            acc, m_new, l_new = cur_ref[...], m_cur, l_cur
        else:
            ml = rows(ml_in, i)
            m_prev, l_prev = ml[:, :LANES], ml[:, LANES:]
            m_new = jnp.maximum(m_prev, m_cur)
            a_prev = jnp.exp(m_prev - m_new)
            a_cur = jnp.exp(m_cur - m_new)
            l_new = a_prev * l_prev + a_cur * l_cur
            acc = (_expand(a_prev, e_ref) * rows(acc_in, i)
                   + _expand(a_cur, e_ref) * cur_ref[...])
        if last:
            o_ref[0, 0, i] = (acc * _expand(1.0 / l_new, e_ref)).reshape(o_ref.shape[3:])
        else:
            acc_out[0, 0, i] = acc.reshape(acc_out.shape[3:])
            ml_out[0, 0, i] = jnp.concatenate([m_new, l_new], axis=1).reshape(ml_out.shape[3:])
        return carry

    lax.fori_loop(0, n_q, one_block, 0)


def _branch_tables(dil, slopes):
    rho = np.arange(BLK)
    if dil == 1:
        idx = RES * (rho % SUBLANES) + rho // SUBLANES
    elif dil == 4:
        idx = 4 * (SUBLANES * (rho // 32) + rho % SUBLANES) + (rho // SUBLANES) % 4
    else:
        assert dil == RES
        idx = rho
    key = np.concatenate([idx - BLK, idx])
    delta = idx[:, None] - key[None, :]
    valid = (delta >= 0) & (delta <= BLK)
    bias = -slopes[:, None, None] * (delta * dil).astype(np.float32)[None]
    tab = np.where(valid[None], bias, np.float32(NEG)).astype(np.float32)
    tab_first = np.where((key >= 0)[None, None, :], tab, np.float32(NEG))
    pairs = lambda a: a.reshape(N_HEADS // 2, 2 * BLK, 2 * BLK)
    return np.concatenate([pairs(tab), pairs(tab_first)], axis=0)


def _head_expander():
    e = np.zeros((LANES, ATTN_WIDTH), np.float32)
    for h in range(N_HEADS):
        e[h, h * HEAD_DIM:(h + 1) * HEAD_DIM] = 1.0
    return jnp.asarray(e, dtype=BF16)


def _attn_prompt(q, kv, state, *, batch, dil, slopes, last):
    n_rows = q.shape[0]
    t = n_rows // batch
    nb = t // dil // BLK
    n_q = min(Q_BLOCKS_PER_STEP, nb)
    first = state is None
    g = RES // dil
    def view(a):
        return a.reshape(batch, nb // n_q, n_q, dil, g, dil, SUBLANES, a.shape[-1])
    def spec(w):
        return pl.BlockSpec((1, 1, n_q, dil, g, 1, SUBLANES, w),
                            lambda b, r, j: (b, j, 0, 0, 0, r, 0, 0))
    def prev_spec(w):
        return pl.BlockSpec((1, 1, 1, dil, g, 1, SUBLANES, w),
                            lambda b, r, j: (b, jnp.maximum(j * n_q - 1, 0), 0, 0, 0, r, 0, 0))
    def prev_view(a):
        return a.reshape(batch, nb, 1, dil, g, dil, SUBLANES, a.shape[-1])
    table = jnp.asarray(_branch_tables(dil, slopes))
    in_specs = [spec(ATTN_WIDTH), prev_spec(2 * ATTN_WIDTH), spec(2 * ATTN_WIDTH),
                _resident(table.shape)]
    args = [view(q), prev_view(kv), view(kv), table]
    if not first:
        in_specs += [_resident((LANES, ATTN_WIDTH)), spec(ATTN_WIDTH), spec(2 * LANES)]
        args += [_head_expander(), view(state[0]), view(state[1])]
    shp = lambda w: jax.ShapeDtypeStruct(
        (batch, nb // n_q, n_q, dil, g, dil, SUBLANES, w), F32)
    if last:
        out_specs, out_shape = spec(ATTN_WIDTH), shp(ATTN_WIDTH)
    else:
        out_specs = [spec(ATTN_WIDTH), spec(2 * LANES)]
        out_shape = [shp(ATTN_WIDTH), shp(2 * LANES)]
    out = pl.pallas_call(
        functools.partial(_attn_prompt_kernel, first=first, last=last, n_q=n_q),
        grid=(batch, dil, nb // n_q),
        in_specs=in_specs,
        out_specs=out_specs,
        out_shape=out_shape,
        scratch_shapes=[pltpu.VMEM(((n_q + 1) * BLK, 2 * ATTN_WIDTH), BF16),
                        pltpu.VMEM((BLK, ATTN_WIDTH), F32)],
        compiler_params=_params(3),
        name=f"attn_prompt_d{dil}",
    )(*args)
    if last:
        return out.reshape(n_rows, ATTN_WIDTH)
    return out[0].reshape(n_rows, ATTN_WIDTH), out[1].reshape(n_rows, 2 * LANES)


def _attn_sample_body(q_ref, kn_ref, vn_ref, ck_ref, cv_ref, tabc_ref, tabn_ref):
    s_new = q_ref.shape[1]
    rows = N_HEADS * s_new
    q = q_ref[0]
    qt = jnp.concatenate([q] * N_HEADS, axis=0)
    r_head = lax.broadcasted_iota(jnp.int32, (rows, ATTN_WIDTH), 0) // s_new
    c_head = lax.broadcasted_iota(jnp.int32, (rows, ATTN_WIDTH), 1) // HEAD_DIM
    own = r_head == c_head
    qm32 = jnp.where(own, qt, 0.0)
    qm = qm32.astype(BF16)
    nt = (((1,), (1,)), ((), ()))
    kc = ck_ref[0, 0].astype(BF16)
    s_c = jnp.dot(qm, kc, preferred_element_type=F32) + tabc_ref[...]
    s_n = lax.dot_general(qm32, kn_ref[0], nt, preferred_element_type=F32) + tabn_ref[...]
    m = jnp.maximum(jnp.max(s_c, axis=-1, keepdims=True),
                    jnp.max(s_n, axis=-1, keepdims=True))
    p_c = jnp.exp(s_c - m)
    p_n = jnp.exp(s_n - m)
    l = jnp.sum(p_c, axis=-1, keepdims=True) + jnp.sum(p_n, axis=-1, keepdims=True)
    o = lax.dot_general(p_c.astype(BF16), cv_ref[0, 0].astype(BF16), nt,
                        preferred_element_type=F32)
    o = o + jnp.dot(p_n, vn_ref[0], preferred_element_type=F32)
    o = jnp.where(own, o / l, 0.0)
    return jnp.sum(o.reshape(N_HEADS, s_new, ATTN_WIDTH), axis=0)


def _sample_tables(r_len, s_new, slopes):
    def table(dist, ok):
        cnt = np.zeros(dist.shape, np.int32)
        for w, d in BRANCHES:
            cnt += (ok & (dist % d == 0) & (dist <= w)).astype(np.int32)
        bias = (-slopes[:, None, None].astype(np.float64) * dist[None]
                + np.log(np.maximum(cnt, 1))[None])
        tab = np.where((cnt > 0)[None], bias, NEG).astype(np.float32)
        return tab.reshape(N_HEADS * s_new, dist.shape[1])
    i = np.arange(s_new)[:, None]
    dist_c = r_len + i - np.arange(r_len)[None, :]
    dist_n = i - np.arange(s_new)[None, :]
    return table(dist_c, dist_c >= 0), table(dist_n, dist_n >= 0)


def _attn_sample_pass(q, kf, vf, cache_k, cache_v, layer, *, slopes):
    depth, b, r_len = cache_k.shape[:3]
    s_new = q.shape[0] // b
    tab_c, tab_n = _sample_tables(r_len, s_new, slopes)
    ck = jnp.transpose(cache_k, (0, 1, 3, 4, 2)).reshape(depth, b, ATTN_WIDTH, r_len)
    cv = jnp.transpose(cache_v, (0, 1, 3, 4, 2)).reshape(depth, b, ATTN_WIDTH, r_len)
    new = pl.BlockSpec((1, s_new, ATTN_WIDTH), lambda i: (i, 0, 0))
    cache = pl.BlockSpec((1, 1, ATTN_WIDTH, r_len), lambda i: (layer, i, 0, 0))
    in_specs = [new, new, new, cache, cache, _resident(tab_c.shape), _resident(tab_n.shape)]
    args = [q.reshape(b, s_new, ATTN_WIDTH), kf.reshape(b, s_new, ATTN_WIDTH),
            vf.reshape(b, s_new, ATTN_WIDTH), ck, cv, jnp.asarray(tab_c), jnp.asarray(tab_n)]
    return in_specs, args, new, jax.ShapeDtypeStruct((b, s_new, ATTN_WIDTH), F32)


def _block_diag(w_pool):
    g, c, _ = w_pool.shape
    out = jnp.zeros((g * c, g * c), w_pool.dtype)
    for gi in range(g):
        out = lax.dynamic_update_slice(out, w_pool[gi], (gi * c, gi * c))
    return out


def kernel(x_prompt, x_sample, cache_k, cache_v, state_pool, g_ffn1, w_gate1, w_up1, w_down1, g_mix, w_in, w_pool, pool_scale, w_out, g_ffn2, w_gate2, w_up2, w_down2, g_final):
    batch, seq, _ = x_prompt.shape
    dec_batch, dec_seq, _ = x_sample.shape
    depth = w_in.shape[0]
    keep = min(WIN_MAX, seq)
    slopes = _alibi_slopes(N_HEADS)
    tm = 512
    seg =1 + POOL_STATE + dec_seq

    xp = _to_granules(x_prompt.reshape(batch * seq, D_MODEL), batch)
    xs = x_sample.reshape(dec_batch * dec_seq, D_MODEL)
    bf = lambda a: a.astype(BF16)
    row = lambda a: a.reshape(1, -1)
    in_order = lambda a: _from_granules(a, batch).reshape(batch, seq, -1)
    kv_prompt = ()
    pp, ksl, vsl, psl = [], [], [], []
    wg1, wu1, wd1 = bf(w_gate1), bf(w_up1), bf(w_down1)
    wg2, wu2, wd2 = bf(w_gate2), bf(w_up2), bf(w_down2)
    wi, wo = bf(w_in), bf(w_out)
    for i in range(depth):
        wpool = bf(_block_diag(w_pool[i]))
        scale = row(pool_scale[i])
        gf = row(g_final) if i == depth - 1 else None

        xs = _ffn(xs, row(g_ffn1[i]), wg1, wu1, wd1, i, tm=tm)
        u, q, _, k, v = _inproj(xs, row(g_mix[i]), wi, i, tm=tm, seq_rows=xs.shape[0],
                                tail_rows=xs.shape[0])
        u3 = u.reshape(dec_batch, dec_seq, POOL_WIDTH)
        ext = jnp.concatenate(
            [jnp.zeros((dec_batch, 1, POOL_WIDTH), F32), state_pool[i], u3], axis=1)
        ext_flat = jnp.pad(ext.reshape(dec_batch * seg, POOL_WIDTH), ((0, POOL_HALO), (0, 0)))
        y_pool_s = _pool_sample(ext_flat, wpool, scale, pos0=PAST_LEN)
        y_pool_s = y_pool_s.reshape(dec_batch, seg, POOL_WIDTH)[:, :dec_seq]
        y_pool_s = y_pool_s.reshape(dec_batch * dec_seq, POOL_WIDTH)
        sample_attn = _attn_sample_pass(q, k, v, cache_k, cache_v, i, slopes=slopes)
        ksl.append(k.reshape(dec_batch, dec_seq, N_HEADS, HEAD_DIM))
        vsl.append(v.reshape(dec_batch, dec_seq, N_HEADS, HEAD_DIM))
        psl.append(ext[:, -POOL_STATE:])

        xp, y_att_s = _ffn(xp, row(g_ffn1[i]), wg1, wu1, wd1, i, tm=xp.shape[0] // dec_batch,
                           hosted=sample_attn)
        u, q, kv, *kv_prompt = _inproj(xp, row(g_mix[i]), wi, i, tm=tm, seq_rows=seq,
                                       tail_rows=keep, returned=kv_prompt)
        y_pool = _pool_prompt(u.reshape(batch, seq, POOL_WIDTH), wpool, scale, tp=tm)
        state = None
        for bi, (_, dil) in enumerate(BRANCHES):
            state = _attn_prompt(q, kv, state, batch=batch, dil=dil, slopes=slopes,
                                 last=bi == len(BRANCHES) - 1)
        xp = _ffn(xp, row(g_ffn2[i]), wg2, wu2, wd2, i, tm=tm, g_final=gf,
                  mixer=(y_pool.reshape(batch * seq, POOL_WIDTH), state, wo))
        u_tail = u.reshape(batch, seq, POOL_WIDTH)[:, -BLK:].reshape(batch * BLK, POOL_WIDTH)
        pp.append(_from_granules(u_tail, batch).reshape(batch, BLK, POOL_WIDTH)[:, -POOL_STATE:])

        xs = _ffn(xs, row(g_ffn2[i]), wg2, wu2, wd2, i, tm=tm, g_final=gf,
                  mixer=(y_pool_s, y_att_s.reshape(dec_batch * dec_seq, ATTN_WIDTH), wo))

    y_prompt = in_order(xp)
    y_sample = xs.reshape(dec_batch, dec_seq, D_MODEL)
    kp, vp = (jnp.transpose(a.reshape(depth, batch, N_HEADS, HEAD_DIM, keep), (0, 1, 4, 2, 3))
              for a in kv_prompt)
    return (y_prompt, y_sample, kp, vp, jnp.stack(pp),
            jnp.stack(ksl), jnp.stack(vsl), jnp.stack(psl))
```
